```python
import jax, jax.numpy as jnp
from jax import lax
import numpy as np

D_MODEL = 1024
BATCH = 4
SEQ = 4096
DEPTH = 4
DEC_BATCH = 32
DEC_SEQ = 1
PAST_LEN = 8192
PAGE_SIZE = 128

HEAD_DIM = 64
H_NSA = 8
KVH_NSA = 2
NSA_REP = H_NSA // KVH_NSA
H_SB = 4
H_FOX = 4
N_HEADS = H_NSA + H_SB + H_FOX
MIX_WIDTH = N_HEADS * HEAD_DIM
CMP_BLOCK = 32
SEL_BLOCK = 64
SEL_TOPK = 16
WINDOW = 512
Q_BLOCK = 128
N_MEM = 256
H_MEM = 4
MEM_WIDTH = H_MEM * HEAD_DIM
D_FF = 4 * D_MODEL
EPS = 1e-6
FORCE_BONUS = 1e4
IN_SIZES = (H_NSA * HEAD_DIM, 2 * KVH_NSA * HEAD_DIM, 2 * KVH_NSA * HEAD_DIM, 2 * KVH_NSA * HEAD_DIM,
            3 * H_NSA, 3 * H_SB * HEAD_DIM, 3 * H_FOX * HEAD_DIM, H_FOX)
IN_WIDTH = sum(IN_SIZES)

kernel_name = 'hymba_nsa_stickbreak_fox_decoder_step'

F32 = jnp.float32


def rmsnorm(x, g):
    xf = x.astype(F32)
    y = xf * lax.rsqrt(jnp.mean(xf * xf, axis=-1, keepdims=True) + EPS)
    return (y * g.astype(F32)).astype(x.dtype)


def norm_k(kv, g):
    return jnp.concatenate([rmsnorm(kv[:, :, :1], g), kv[:, :, 1:]], axis=2)


def split_cols(h, sizes):
    out, off = [], 0
    for n in sizes:
        out.append(h[..., off:off + n])
        off += n
    return out


def alibi_slopes(n):
    return jnp.exp2(-8.0 * jnp.arange(1, n + 1, dtype=F32) / n)


def masked_softmax(s, mask):
    s = jnp.where(mask, s, -jnp.inf)
    m = jnp.max(s, axis=-1, keepdims=True)
    m = jnp.where(jnp.isfinite(m), m, 0.0)
    p = jnp.exp(s - m)
    return p / jnp.maximum(jnp.sum(p, axis=-1, keepdims=True), 1e-30)


def project_mixers(xn, w_in, b_gate, b_forget, g_qk):
    B, T, _ = xn.shape
    q_nsa, kv_cmp, kv_sel, kv_win, gate_lg, qkv_sb, qkv_fox, f_lg = split_cols(xn @ w_in, IN_SIZES)
    q_nsa = rmsnorm(q_nsa.reshape(B, T, H_NSA, HEAD_DIM), g_qk[0])
    kv_cmp = kv_cmp.reshape(B, T, 2, KVH_NSA, HEAD_DIM)
    kv_sel = norm_k(kv_sel.reshape(B, T, 2, KVH_NSA, HEAD_DIM), g_qk[2])
    kv_win = norm_k(kv_win.reshape(B, T, 2, KVH_NSA, HEAD_DIM), g_qk[3])
    gates = jax.nn.sigmoid((gate_lg + b_gate).astype(F32)).reshape(B, T, H_NSA, 3)
    qkv_sb = qkv_sb.reshape(B, T, 3, H_SB, HEAD_DIM)
    q_sb, kv_sb = qkv_sb[:, :, 0], qkv_sb[:, :, 1:]
    qkv_fox = qkv_fox.reshape(B, T, 3, H_FOX, HEAD_DIM)
    q_fox = rmsnorm(qkv_fox[:, :, 0], g_qk[4])
    kv_fox = norm_k(qkv_fox[:, :, 1:], g_qk[5])
    logf = jax.nn.log_sigmoid((f_lg + b_forget).astype(F32))
    return (q_nsa, gates, q_sb, q_fox, kv_cmp, kv_sel, kv_win, kv_sb, kv_fox, logf)


def compress_blocks(kv_rows, w_cmp, g_k):
    B, L = kv_rows.shape[:2]
    blk = kv_rows.reshape(B, L // CMP_BLOCK, CMP_BLOCK, 2, KVH_NSA, HEAD_DIM)
    c = jnp.einsum('bnicgd,cig->bncgd', blk, w_cmp)
    return rmsnorm(c[:, :, 0], g_k), c[:, :, 1]


def nsa_attend(q, q_pos, gates, kc, vc, ks, vs, kw, vw, w_pos, slopes):
    B, Tq = q.shape[:2]
    qg = q.reshape(B, Tq, KVH_NSA, NSA_REP, HEAD_DIM).astype(F32)
    scale = HEAD_DIM ** -0.5
    slope = slopes.reshape(KVH_NSA, NSA_REP)[:, :, None, None]
    nbc = kc.shape[1]
    c_end = jnp.arange(nbc) * CMP_BLOCK + (CMP_BLOCK - 1)
    dc = q_pos[:, None] - c_end[None, :]
    s_c = jnp.einsum('bqgrd,bngd->bgrqn', qg, kc.astype(F32)) * scale - slope * dc.astype(F32)
    p_c = masked_softmax(s_c, dc >= 0)
    o_c = jnp.einsum('bgrqn,bngd->bqgrd', p_c, vc.astype(F32))
    nbs = ks.shape[1] // SEL_BLOCK
    imp = p_c.sum(2).reshape(B, KVH_NSA, Tq, nbs, SEL_BLOCK // CMP_BLOCK).sum(-1)
    blk = jnp.arange(nbs)[None, :]
    cur = (q_pos // SEL_BLOCK)[:, None]
    forced = ((blk == 0) | (blk == cur) | (blk == cur - 1)).astype(F32)
    score = jnp.where(blk * SEL_BLOCK <= q_pos[:, None], imp + FORCE_BONUS * forced, -jnp.inf)
    k_top = min(SEL_TOPK, nbs)
    _, idx = lax.top_k(score, k_top)

    def gather_blocks(rows):
        blocks = rows.reshape(B, nbs, SEL_BLOCK, KVH_NSA, HEAD_DIM).transpose(0, 3, 1, 2, 4)
        g = jax.vmap(jax.vmap(lambda bl, ix: bl[ix]))(blocks, idx)
        return g.reshape(B, KVH_NSA, Tq, k_top * SEL_BLOCK, HEAD_DIM).astype(F32)

    k_sel, v_sel = gather_blocks(ks), gather_blocks(vs)
    pos_sel = (idx[..., None] * SEL_BLOCK + jnp.arange(SEL_BLOCK)).reshape(B, KVH_NSA, Tq, k_top * SEL_BLOCK)
    ds = (q_pos[None, None, :, None] - pos_sel)[:, :, None]
    s_s = jnp.einsum('bqgrd,bgqkd->bgrqk', qg, k_sel) * scale - slope[None] * ds.astype(F32)
    p_s = masked_softmax(s_s, ds >= 0)
    o_s = jnp.einsum('bgrqk,bgqkd->bqgrd', p_s, v_sel)
    dw = q_pos[:, None] - w_pos[None, :]
    s_w = jnp.einsum('bqgrd,bkgd->bgrqk', qg, kw.astype(F32)) * scale - slope * dw.astype(F32)
    p_w = masked_softmax(s_w, (dw >= 0) & (dw <= WINDOW) & (w_pos[None, :] >= 0))
    o_w = jnp.einsum('bgrqk,bkgd->bqgrd', p_w, vw.astype(F32))
    g = gates.reshape(B, Tq, KVH_NSA, NSA_REP, 3)
    out = g[..., 0:1] * o_c + g[..., 1:2] * o_s + g[..., 2:3] * o_w
    return out.reshape(B, Tq, H_NSA, HEAD_DIM)


def stick_breaking_attend(q, k, v, q_pos, k_pos):
    z = jnp.einsum('bqhd,bkhd->bhqk', q.astype(F32), k.astype(F32)) * HEAD_DIM ** -0.5
    mask = k_pos[None, :] < q_pos[:, None]
    log_keep = jnp.where(mask, jax.nn.log_sigmoid(-z), 0.0)
    after = lax.cumsum(log_keep, axis=3, reverse=True) - log_keep
    a = jnp.where(mask, jnp.exp(jax.nn.log_sigmoid(z) + after), 0.0)
    return jnp.einsum('bhqk,bkhd->bqhd', a, v.astype(F32))


def forgetting_attend(q, k, v, cq, ck, q_pos, k_pos):
    s = jnp.einsum('bqhd,bkhd->bhqk', q.astype(F32), k.astype(F32)) * HEAD_DIM ** -0.5
    s = s + jnp.transpose(cq, (0, 2, 1))[..., None] - jnp.transpose(ck, (0, 2, 1))[:, :, None, :]
    p = masked_softmax(s, k_pos[None, :] <= q_pos[:, None])
    return jnp.einsum('bhqk,bkhd->bqhd', p, v.astype(F32))


def mix_queries(q_pos, q_nsa, gates, q_sb, q_fox, cq, kc, vc, ks, vs, kw, vw, w_pos,
                k_sb, v_sb, k_fox, v_fox, ck, k_pos, slopes):
    o_nsa = nsa_attend(q_nsa, q_pos, gates, kc, vc, ks, vs, kw, vw, w_pos, slopes)
    o_sb = stick_breaking_attend(q_sb, k_sb, v_sb, q_pos, k_pos)
    o_fox = forgetting_attend(q_fox, k_fox, v_fox, cq, ck, q_pos, k_pos)
    return jnp.concatenate([o_nsa, o_sb, o_fox], axis=2)


def prompt_token_mix(proj, w_cmp, g_kcmp, slopes):
    q_nsa, gates, q_sb, q_fox, kv_cmp, kv_sel, kv_win, kv_sb, kv_fox, logf = proj
    B, T = q_nsa.shape[:2]
    kc, vc = compress_blocks(kv_cmp, w_cmp, g_kcmp)
    kv_win_pad = jnp.pad(kv_win, ((0, 0), (WINDOW, 0), (0, 0), (0, 0), (0, 0)))
    c = jnp.cumsum(logf, axis=1)
    k_pos = jnp.arange(T)
    ks, vs = kv_sel[:, :, 0], kv_sel[:, :, 1]
    k_sb, v_sb = kv_sb[:, :, 0], kv_sb[:, :, 1]
    k_fox, v_fox = kv_fox[:, :, 0], kv_fox[:, :, 1]

    def block(q0):
        sl = lambda a: lax.dynamic_slice_in_dim(a, q0, Q_BLOCK, axis=1)
        q_pos = q0 + jnp.arange(Q_BLOCK)
        win = lax.dynamic_slice_in_dim(kv_win_pad, q0, WINDOW + Q_BLOCK, axis=1)
        w_pos = q0 - WINDOW + jnp.arange(WINDOW + Q_BLOCK)
        return mix_queries(q_pos, sl(q_nsa), sl(gates), sl(q_sb), sl(q_fox), sl(c), kc, vc, ks, vs,
                           win[:, :, 0], win[:, :, 1], w_pos, k_sb, v_sb, k_fox, v_fox, c, k_pos, slopes)

    o = lax.map(block, jnp.arange(0, T, Q_BLOCK))
    return jnp.moveaxis(o, 0, 1).reshape(B, T, N_HEADS, HEAD_DIM)


def sample_token_mix(proj, past_cmp, past_sel, win_buf, past_sb, past_fox, past_logf, w_cmp, g_kcmp, slopes):
    q_nsa, gates, q_sb, q_fox, kv_cmp, kv_sel, kv_win, kv_sb, kv_fox, logf = proj
    Tn = q_nsa.shape[1]
    P = past_cmp.shape[1]
    L = P + Tn
    Lp = -(-L // SEL_BLOCK) * SEL_BLOCK
    pad_len = lambda a: jnp.pad(a, [(0, 0), (0, Lp - L)] + [(0, 0)] * (a.ndim - 2))
    cmp_all = pad_len(jnp.concatenate([past_cmp, kv_cmp], axis=1))
    sel_all = pad_len(jnp.concatenate([past_sel, kv_sel], axis=1))
    kc, vc = compress_blocks(cmp_all, w_cmp, g_kcmp)
    n_buf = win_buf.shape[1]
    win_all = jnp.concatenate([win_buf, kv_win], axis=1)
    w_pos = P - n_buf + jnp.arange(n_buf + Tn)
    sb_all = jnp.concatenate([past_sb, kv_sb], axis=1)
    fox_all = jnp.concatenate([past_fox, kv_fox], axis=1)
    c = jnp.cumsum(jnp.concatenate([past_logf.astype(F32), logf], axis=1), axis=1)
    k_pos = jnp.arange(L)
    q_pos = P + jnp.arange(Tn)
    o = mix_queries(q_pos, q_nsa, gates, q_sb, q_fox, c[:, P:], kc, vc, sel_all[:, :, 0], sel_all[:, :, 1],
                    win_all[:, :, 0], win_all[:, :, 1], w_pos, sb_all[:, :, 0], sb_all[:, :, 1],
                    fox_all[:, :, 0], fox_all[:, :, 1], c, k_pos, slopes)
    return o, win_all[:, Tn:]


def mixer_output(o, g_head, w_out, dtype):
    B, T = o.shape[:2]
    o = rmsnorm(o, g_head.reshape(N_HEADS, HEAD_DIM))
    return o.reshape(B, T, MIX_WIDTH).astype(dtype) @ w_out


def memory_kv(mem, w_kv, g_k):
    B = mem.shape[0]
    return norm_k((mem @ w_kv).reshape(B, N_MEM, 2, H_MEM, HEAD_DIM), g_k)


def memory_attend(xn, kv_mem, w_q, g_q, w_o):
    B, T, _ = xn.shape
    q = rmsnorm((xn @ w_q).reshape(B, T, H_MEM, HEAD_DIM), g_q).astype(F32)
    s = jnp.einsum('bqhd,bmhd->bhqm', q, kv_mem[:, :, 0].astype(F32)) * HEAD_DIM ** -0.5
    p = jax.nn.softmax(s, axis=-1)
    o = jnp.einsum('bhqm,bmhd->bqhd', p, kv_mem[:, :, 1].astype(F32))
    return o.reshape(B, T, MEM_WIDTH).astype(xn.dtype) @ w_o


def sq_relu_mlp(xn, w_up, w_down):
    h = jax.nn.relu(xn @ w_up)
    return (h * h) @ w_down


def setup_inputs(seed: int = 0) -> dict:
    key = jax.random.key(seed)
    ks = jax.random.split(key, 32)
    n_pages = PAST_LEN // PAGE_SIZE
    n_pool = (DEC_BATCH * n_pages * 5) // 4
    w_buf = min(WINDOW, PAST_LEN)
    nrm = lambda k, shape, s=1.0: s * jax.random.normal(k, shape, F32)
    gain = lambda k, shape: 1.0 + 0.1 * jax.random.normal(k, shape, F32)
    page_table = jax.random.permutation(ks[0], n_pool)[: DEC_BATCH * n_pages].reshape(DEC_BATCH, n_pages).astype(jnp.int32)
    return {
        'x_prompt': nrm(ks[1], (BATCH, SEQ, D_MODEL)),
        'x_sample': nrm(ks[2], (DEC_BATCH, DEC_SEQ, D_MODEL)),
        'mem_prompt': nrm(ks[3], (BATCH, N_MEM, D_MODEL)),
        'cache_nsa_cmp': nrm(ks[4], (DEPTH, n_pool, PAGE_SIZE, 2, KVH_NSA, HEAD_DIM)),
        'cache_nsa_sel': nrm(ks[5], (DEPTH, n_pool, PAGE_SIZE, 2, KVH_NSA, HEAD_DIM)),
        'cache_nsa_win': nrm(ks[6], (DEPTH, DEC_BATCH, w_buf, 2, KVH_NSA, HEAD_DIM)),
        'cache_sb': nrm(ks[7], (DEPTH, n_pool, PAGE_SIZE, 2, H_SB, HEAD_DIM)),
        'cache_fox': nrm(ks[8], (DEPTH, n_pool, PAGE_SIZE, 2, H_FOX, HEAD_DIM)),
        'cache_fox_logf': jax.nn.log_sigmoid(3.0 + nrm(ks[9], (DEPTH, n_pool, PAGE_SIZE, H_FOX))),
        'cache_mem': nrm(ks[10], (DEPTH, DEC_BATCH, N_MEM, 2, H_MEM, HEAD_DIM)),
        'page_table': page_table,
        'g_mix_norm': gain(ks[11], (DEPTH, D_MODEL)),
        'w_in': nrm(ks[12], (DEPTH, D_MODEL, IN_WIDTH), D_MODEL ** -0.5),
        'b_nsa_gate': nrm(ks[13], (DEPTH, 3 * H_NSA), 0.1),
        'b_forget': 3.0 + nrm(ks[14], (DEPTH, H_FOX), 0.5),
        'g_qk': gain(ks[15], (DEPTH, 6, HEAD_DIM)),
        'w_cmp': (1.0 / CMP_BLOCK) * gain(ks[16], (DEPTH, 2, CMP_BLOCK, KVH_NSA)),
        'g_head_out': gain(ks[17], (DEPTH, MIX_WIDTH)),
        'w_out': nrm(ks[18], (DEPTH, MIX_WIDTH, D_MODEL), MIX_WIDTH ** -0.5),
        'g_mem_norm': gain(ks[19], (DEPTH, D_MODEL)),
        'w_mem_q': nrm(ks[20], (DEPTH, D_MODEL, MEM_WIDTH), D_MODEL ** -0.5),
        'w_mem_kv': nrm(ks[21], (DEPTH, D_MODEL, 2 * MEM_WIDTH), D_MODEL ** -0.5),
        'g_mem_qk': gain(ks[22], (DEPTH, 2, HEAD_DIM)),
        'w_mem_o': nrm(ks[23], (DEPTH, MEM_WIDTH, D_MODEL), MEM_WIDTH ** -0.5),
        'g_mlp_norm': gain(ks[24], (DEPTH, D_MODEL)),
        'w_up': nrm(ks[25], (DEPTH, D_MODEL, D_FF), D_MODEL ** -0.5),
        'w_down': nrm(ks[26], (DEPTH, D_FF, D_MODEL), D_FF ** -0.5),
    }


def reference(x_prompt, x_sample, mem_prompt, cache_nsa_cmp, cache_nsa_sel, cache_nsa_win, cache_sb, cache_fox,
              cache_fox_logf, cache_mem, page_table, g_mix_norm, w_in, b_nsa_gate, b_forget, g_qk, w_cmp,
              g_head_out, w_out, g_mem_norm, w_mem_q, w_mem_kv, g_mem_qk, w_mem_o, g_mlp_norm, w_up, w_down):
    slopes = alibi_slopes(H_NSA)

    def gather_pages(pool):
        g = pool[page_table]
        return g.reshape((g.shape[0], g.shape[1] * g.shape[2]) + g.shape[3:])

    xp, xs = x_prompt, x_sample
    T = xp.shape[1]
    cmp_p, sel_p, win_p, sb_p, fox_p, logf_p, mem_p = [], [], [], [], [], [], []
    cmp_s, sel_s, win_s, sb_s, fox_s, logf_s = [], [], [], [], [], []
    for l in range(DEPTH):
        pp = project_mixers(rmsnorm(xp, g_mix_norm[l]), w_in[l], b_nsa_gate[l], b_forget[l], g_qk[l])
        ps = project_mixers(rmsnorm(xs, g_mix_norm[l]), w_in[l], b_nsa_gate[l], b_forget[l], g_qk[l])
        op = prompt_token_mix(pp, w_cmp[l], g_qk[l, 1], slopes)
        os_, new_win = sample_token_mix(ps, gather_pages(cache_nsa_cmp[l]), gather_pages(cache_nsa_sel[l]),
                                        cache_nsa_win[l], gather_pages(cache_sb[l]), gather_pages(cache_fox[l]),
                                        gather_pages(cache_fox_logf[l]), w_cmp[l], g_qk[l, 1], slopes)
        xp = xp + mixer_output(op, g_head_out[l], w_out[l], xp.dtype)
        xs = xs + mixer_output(os_, g_head_out[l], w_out[l], xs.dtype)
        kvm_p = memory_kv(mem_prompt, w_mem_kv[l], g_mem_qk[l, 1])
        xp = xp + memory_attend(rmsnorm(xp, g_mem_norm[l]), kvm_p, w_mem_q[l], g_mem_qk[l, 0], w_mem_o[l])
        xs = xs + memory_attend(rmsnorm(xs, g_mem_norm[l]), cache_mem[l], w_mem_q[l], g_mem_qk[l, 0], w_mem_o[l])
        xp = xp + sq_relu_mlp(rmsnorm(xp, g_mlp_norm[l]), w_up[l], w_down[l])
        xs = xs + sq_relu_mlp(rmsnorm(xs, g_mlp_norm[l]), w_up[l], w_down[l])
        cmp_p.append(pp[4]); sel_p.append(pp[5]); win_p.append(pp[6][:, T - min(WINDOW, T):])
        sb_p.append(pp[7]); fox_p.append(pp[8]); logf_p.append(pp[9]); mem_p.append(kvm_p)
        cmp_s.append(ps[4]); sel_s.append(ps[5]); win_s.append(new_win)
        sb_s.append(ps[7]); fox_s.append(ps[8]); logf_s.append(ps[9])
    st = lambda a: jnp.stack(a, axis=0)
    return (xp, xs, st(cmp_p), st(sel_p), st(win_p), st(sb_p), st(fox_p), st(logf_p), st(mem_p),
            st(cmp_s), st(sel_s), st(win_s), st(sb_s), st(fox_s), st(logf_s))
```

```python
import functools

import jax
import jax.numpy as jnp
from jax import lax
from jax.experimental import pallas as pl
from jax.experimental.pallas import tpu as pltpu

F32 = jnp.float32
BF16 = jnp.bfloat16

D_MODEL = 1024
HEAD_DIM = 64
LANES = 128
H_NSA, KVH_NSA, NSA_REP = 8, 2, 4
H_SB, H_FOX, H_MEM = 4, 4, 4
CMP_BLOCK, SEL_BLOCK, SEL_TOPK, WINDOW = 32, 64, 16, 512
N_MEM = 256
D_FF = 4 * D_MODEL
EPS = 1e-6
FORCE_BONUS = 1e4
SCALE = HEAD_DIM ** -0.5
NEG = -1e30
VMEM_LIMIT = 48 * 1024 * 1024

C_Q, C_CMP, C_SEL, C_WIN, C_SB, C_FOX, C_MISC = 0, 512, 768, 1024, 1280, 2048, 2816
IN_PAD = 2944
GATE_COLS = 3 * H_NSA


def _cparams(sem):
    return pltpu.CompilerParams(dimension_semantics=sem, vmem_limit_bytes=VMEM_LIMIT)


def _dot(a, b):
    return jnp.dot(a, b, preferred_element_type=F32)


def _dot_nt(a, b):
    return lax.dot_general(a, b, (((1,), (1,)), ((), ())), preferred_element_type=F32)


def _split2(x):
    hi = x.astype(BF16)
    lo = (x - hi.astype(F32)).astype(BF16)
    return hi, lo


def _split3(x):
    hi = x.astype(BF16)
    r = x - hi.astype(F32)
    mid = r.astype(BF16)
    lo = (r - mid.astype(F32)).astype(BF16)
    return hi, mid, lo


def _dot_f32(x, m):
    hi, lo = _split2(x)
    return _dot(hi, m) + _dot(lo, m)


def _dot_f32_left(m, x):
    hi, mid, lo = _split3(x)
    return _dot(m, hi) + _dot(m, mid) + _dot(m, lo)


def _head_rms(y, seg, g):
    ss = _dot_f32(y * y, seg)
    return y * lax.rsqrt(ss * (1.0 / HEAD_DIM) + EPS) * g


def _row_rms(x, g):
    return x * lax.rsqrt(jnp.mean(x * x, axis=-1, keepdims=True) + EPS) * g


def _log_sigmoid(z):
    return jnp.minimum(z, 0.0) - jnp.log(1.0 + jnp.exp(-jnp.abs(z)))


def _sigmoid(z):
    return 1.0 / (1.0 + jnp.exp(-z))


def _lane_lt64(shape):
    return lax.broadcasted_iota(jnp.int32, shape, len(shape) - 1) < HEAD_DIM


def _seg_matrix(n):
    i = jnp.arange(n) // HEAD_DIM
    return (i[:, None] == i[None, :]).astype(BF16)


def _strict_upper(n):
    i = jnp.arange(n)
    return (i[:, None] > i[None, :]).astype(BF16)


def _lower_incl(n):
    i = jnp.arange(n)
    return (i[None, :] <= i[:, None]).astype(BF16)


def _in_proj_kernel(x_ref, gn_ref, w_ref, bias_ref, gq_ref, gsel_ref, gwin_ref, gfq_ref, gfk_ref, seg_ref,
                    q_ref, cmp_ref, sel_ref, selb_ref, win_ref, winb_ref, misc_ref,
                    qsb_ref, sb_ref, sbb_ref, qfox_ref, fox_ref, foxb_ref):
    xn = _row_rms(x_ref[...], gn_ref[...]).astype(BF16)
    seg = seg_ref[...]
    seg1 = seg[:LANES, :LANES]

    def proj(c0, width):
        return _dot(xn, w_ref[:, c0:c0 + width])

    for j in range(2):
        q = proj(C_Q + 256 * j, 256)
        q_ref[:, 256 * j:256 * (j + 1)] = (_head_rms(q, seg, gq_ref[...]) * SCALE).astype(BF16)

    cmp_ref[...] = proj(C_CMP, 256)

    for c0, g_ref, o_ref, ob_ref in ((C_SEL, gsel_ref, sel_ref, selb_ref), (C_WIN, gwin_ref, win_ref, winb_ref)):
        kv = proj(c0, 256)
        k = _head_rms(kv[:, :LANES], seg1, g_ref[...])
        o_ref[:, :LANES] = k
        o_ref[:, LANES:] = kv[:, LANES:]
        ob_ref[:, :LANES] = k.astype(BF16)
        ob_ref[:, LANES:] = kv[:, LANES:].astype(BF16)

    z = proj(C_MISC, LANES) + bias_ref[...]
    lane = lax.broadcasted_iota(jnp.int32, z.shape, 1)
    misc_ref[...] = jnp.where(lane < GATE_COLS, _sigmoid(z), _log_sigmoid(z))

    qkv = proj(C_SB, 768)
    qsb_ref[...] = (qkv[:, :256] * SCALE).astype(BF16)
    sb_ref[...] = qkv[:, 256:]
    sbb_ref[...] = qkv[:, 256:].astype(BF16)

    qkv = proj(C_FOX, 768)
    qfox_ref[...] = (_head_rms(qkv[:, :256], seg, gfq_ref[...]) * SCALE).astype(BF16)
    k = _head_rms(qkv[:, 256:512], seg, gfk_ref[...])
    fox_ref[:, :256] = k
    fox_ref[:, 256:] = qkv[:, 512:]
    foxb_ref[:, :256] = k.astype(BF16)
    foxb_ref[:, 256:] = qkv[:, 512:].astype(BF16)


def _in_proj(x, gn, w, bias, gq, gsel, gwin, gfq, gfk, seg, tm):
    n = x.shape[0]
    row = lambda width: pl.BlockSpec((tm, width), lambda i: (i, 0))
    const = lambda a: pl.BlockSpec(a.shape, lambda i: (0, 0))
    widths = ((512, BF16), (256, F32), (256, F32), (256, BF16), (256, F32), (256, BF16), (LANES, F32),
              (256, BF16), (512, F32), (512, BF16), (256, BF16), (512, F32), (512, BF16))
    consts = (gn, w, bias, gq, gsel, gwin, gfq, gfk, seg)
    return pl.pallas_call(
        _in_proj_kernel,
        grid=(n // tm,),
        in_specs=[row(D_MODEL)] + [const(a) for a in consts],
        out_specs=[row(wd) for wd, _ in widths],
        out_shape=[jax.ShapeDtypeStruct((n, wd), dt) for wd, dt in widths],
        compiler_params=_cparams(("parallel",)),
        name="in_proj",
    )(x, *consts)


def _compress_kernel(x_ref, w_ref, gk_ref, seg_ref, kc_ref, vc_ref):
    rows = x_ref.shape[0]
    x = x_ref[...].reshape(rows // CMP_BLOCK, CMP_BLOCK, 4 * HEAD_DIM)
    c = jnp.sum(x * w_ref[...][None], axis=1)
    kc_ref[...] = _head_rms(c[:, :LANES], seg_ref[...], gk_ref[...]).astype(BF16)
    vc_ref[...] = c[:, LANES:].astype(BF16)


def _compress(kv_cmp, w_rows, gk, seg1, tr):
    n = kv_cmp.shape[0]
    nb = tr // CMP_BLOCK
    out = jax.ShapeDtypeStruct((n // CMP_BLOCK, LANES), BF16)
    return pl.pallas_call(
        _compress_kernel,
        grid=(n // tr,),
        in_specs=[pl.BlockSpec((tr, 256), lambda i: (i, 0)),
                  pl.BlockSpec(w_rows.shape, lambda i: (0, 0)),
                  pl.BlockSpec(gk.shape, lambda i: (0, 0)),
                  pl.BlockSpec(seg1.shape, lambda i: (0, 0))],
        out_specs=[pl.BlockSpec((nb, LANES), lambda i: (i, 0))] * 2,
        out_shape=[out, out],
        compiler_params=_cparams(("parallel",)),
        name="compress",
    )(kv_cmp, w_rows, gk, seg1)


TQ_NSA = 128
TK_SEL = 512


def _softmax_step(s, mask, v_b, m_ref, l_ref, acc_ref, p_ref, heads, rows):
    for h in range(heads):
        r0 = h * rows
        sh = jnp.where(mask[h], s[h], NEG)
        m_old = m_ref[r0:r0 + rows]
        m_new = jnp.maximum(m_old, jnp.max(sh, axis=1, keepdims=True))
        p = jnp.where(mask[h], jnp.exp(sh - m_new), 0.0)
        alpha = jnp.exp(m_old - m_new)
        l_ref[r0:r0 + rows] = alpha * l_ref[r0:r0 + rows] + jnp.sum(p, axis=1, keepdims=True)
        acc_ref[r0:r0 + rows] = alpha * acc_ref[r0:r0 + rows]
        m_ref[r0:r0 + rows] = m_new
        p_ref[r0:r0 + rows, :p.shape[1]] = p.astype(BF16)
    tk = v_b.shape[0]
    acc_ref[...] += _dot(p_ref[:, :tk], v_b)


def _softmax_finish(m_ref, l_ref, acc_ref):
    return acc_ref[...] / jnp.maximum(l_ref[...], 1e-30)


def _softmax_reset(m_ref, l_ref, acc_ref):
    m_ref[...] = jnp.full(m_ref.shape, NEG, F32)
    l_ref[...] = jnp.zeros(l_ref.shape, F32)
    acc_ref[...] = jnp.zeros(acc_ref.shape, F32)


def _nsa_slope(h):
    return 2.0 ** (-(h + 1))


def _nsa_prompt_kernel(q_ref, gate_ref, kc_ref, vc_ref, ks_ref, vs_ref, kw_ref, vw_ref, pair_ref, exp_ref,
                       o_ref, q8_ref, m_ref, l_ref, acc_ref, p_ref, oc_ref, selm_ref):
    i = pl.program_id(1)
    tq = TQ_NSA
    q0 = i * tq
    nbc = kc_ref.shape[1]
    nbs = nbc // 2
    low = _lane_lt64((tq, LANES))
    for g in range(KVH_NSA):
        for r in range(NSA_REP):
            qt = q_ref[:, r * LANES:(r + 1) * LANES]
            keep = low if g == 0 else jnp.logical_not(low)
            q8_ref[(g * NSA_REP + r) * tq:(g * NSA_REP + r + 1) * tq, :] = jnp.where(keep, qt, jnp.zeros_like(qt))
    q8 = q8_ref[...]
    qpos = q0 + lax.broadcasted_iota(jnp.int32, (tq, 1), 0)

    s_all = _dot_nt(q8, kc_ref[0])
    blk = lax.broadcasted_iota(jnp.int32, (tq, nbc), 1)
    dc = qpos - (blk * CMP_BLOCK + (CMP_BLOCK - 1))
    vis = dc >= 0
    dcf = dc.astype(F32)
    for g in range(KVH_NSA):
        imp_c = jnp.zeros((tq, nbc), F32)
        for r in range(NSA_REP):
            h = g * NSA_REP + r
            s = jnp.where(vis, s_all[h * tq:(h + 1) * tq] - _nsa_slope(h) * dcf, NEG)
            m = jnp.max(s, axis=1, keepdims=True)
            p = jnp.where(vis, jnp.exp(s - m), 0.0)
            p = p / jnp.maximum(jnp.sum(p, axis=1, keepdims=True), 1e-30)
            imp_c = imp_c + p
            p_ref[h * tq:(h + 1) * tq, :nbc] = p.astype(BF16)
        imp = _dot_f32(imp_c, pair_ref[...])
        lane = lax.broadcasted_iota(jnp.int32, (tq, LANES), 1)
        cur = jnp.right_shift(qpos, 6)
        forced = (lane == 0) | (lane == cur) | (lane == cur - 1)
        score = jnp.where((lane * SEL_BLOCK <= qpos) & (lane < nbs),
                          imp + jnp.where(forced, FORCE_BONUS, 0.0), NEG)
        lanef = lane.astype(F32)
        sel = jnp.zeros((tq, LANES), F32)
        for _ in range(min(SEL_TOPK, nbs)):
            best = jnp.max(score, axis=1, keepdims=True)
            first = jnp.min(jnp.where(score == best, lanef, 1e9), axis=1, keepdims=True)
            hit = lanef == first
            sel = jnp.where(hit, 1.0, sel)
            score = jnp.where(hit, NEG, score)
        selm_ref[g] = sel.astype(BF16)
    oc_ref[...] = _dot(p_ref[:, :nbc], vc_ref[0])

    def branch_out():
        return _softmax_finish(m_ref, l_ref, acc_ref)

    _softmax_reset(m_ref, l_ref, acc_ref)

    def sel_tile(kt, carry):
        k0 = pl.multiple_of(kt * TK_SEL, TK_SEL)
        s_all = _dot_nt(q8, ks_ref[pl.ds(k0, TK_SEL), :])
        kpos = k0 + lax.broadcasted_iota(jnp.int32, (tq, TK_SEL), 1)
        dist = qpos - kpos
        distf = dist.astype(F32)
        s, mask = [], []
        for g in range(KVH_NSA):
            chosen = _dot(selm_ref[g], exp_ref[kt])
            mk = jnp.where(dist >= 0, chosen, 0.0) > 0.5
            for r in range(NSA_REP):
                h = g * NSA_REP + r
                s.append(s_all[h * tq:(h + 1) * tq] - _nsa_slope(h) * distf)
                mask.append(mk)
        _softmax_step(s, mask, vs_ref[pl.ds(k0, TK_SEL), :], m_ref, l_ref, acc_ref, p_ref, H_NSA, tq)
        return carry

    lax.fori_loop(0, (q0 + tq - 1) // TK_SEL + 1, sel_tile, 0)
    o_sel = branch_out()

    _softmax_reset(m_ref, l_ref, acc_ref)

    def win_tile(kt, carry):
        k0 = pl.multiple_of(kt * tq, tq)
        s_all = _dot_nt(q8, kw_ref[pl.ds(k0, tq), :])
        kpos = k0 + lax.broadcasted_iota(jnp.int32, (tq, tq), 1)
        dist = qpos - kpos
        distf = dist.astype(F32)
        mk = jnp.where(dist >= 0, dist, WINDOW + 1) <= WINDOW
        s = [s_all[h * tq:(h + 1) * tq] - _nsa_slope(h) * distf for h in range(H_NSA)]
        _softmax_step(s, [mk] * H_NSA, vw_ref[pl.ds(k0, tq), :], m_ref, l_ref, acc_ref, p_ref, H_NSA, tq)
        return carry

    lax.fori_loop(jnp.maximum(i - WINDOW // tq, 0), i + 1, win_tile, 0)
    o_win = branch_out()

    gates = gate_ref[...]
    oc = oc_ref[...]
    for r in range(NSA_REP):
        halves = []
        for g in range(KVH_NSA):
            h = g * NSA_REP + r
            rows = slice(h * tq, (h + 1) * tq)
            halves.append(gates[:, 3 * h:3 * h + 1] * oc[rows] + gates[:, 3 * h + 1:3 * h + 2] * o_sel[rows]
                          + gates[:, 3 * h + 2:3 * h + 3] * o_win[rows])
        o_ref[:, r * LANES:(r + 1) * LANES] = jnp.where(low, halves[0], halves[1])


def _nsa_prompt(q, misc, kc, vc, selb, winb, pair, expand, batch, seq):
    nq = seq // TQ_NSA
    nbc = seq // CMP_BLOCK
    rows = H_NSA * TQ_NSA
    qmap = lambda b, i: (b * nq + i, 0)
    return pl.pallas_call(
        _nsa_prompt_kernel,
        grid=(batch, nq),
        in_specs=[pl.BlockSpec((TQ_NSA, 512), qmap),
                  pl.BlockSpec((TQ_NSA, LANES), qmap),
                  pl.BlockSpec((1, nbc, LANES), lambda b, i: (b, 0, 0)),
                  pl.BlockSpec((1, nbc, LANES), lambda b, i: (b, 0, 0)),
                  pl.BlockSpec((seq, LANES), lambda b, i: (b, 0)),
                  pl.BlockSpec((seq, LANES), lambda b, i: (b, 1)),
                  pl.BlockSpec((seq, LANES), lambda b, i: (b, 0)),
                  pl.BlockSpec((seq, LANES), lambda b, i: (b, 1)),
                  pl.BlockSpec(pair.shape, lambda b, i: (0, 0)),
                  pl.BlockSpec(expand.shape, lambda b, i: (0, 0, 0))],
        out_specs=pl.BlockSpec((TQ_NSA, 512), qmap),
        out_shape=jax.ShapeDtypeStruct((batch * seq, 512), F32),
        scratch_shapes=[pltpu.VMEM((rows, LANES), BF16),
                        pltpu.VMEM((rows, 1), F32), pltpu.VMEM((rows, 1), F32),
                        pltpu.VMEM((rows, LANES), F32),
                        pltpu.VMEM((rows, TK_SEL), BF16),
                        pltpu.VMEM((rows, LANES), F32),
                        pltpu.VMEM((KVH_NSA, TQ_NSA, LANES), BF16)],
        compiler_params=_cparams(("parallel", "arbitrary")),
        name="nsa_prompt",
    )(q, misc, kc, vc, selb, selb, winb, winb, pair, expand)


TQ_ATT = 256


def _lane_head(shape):
    return jnp.right_shift(lax.broadcasted_iota(jnp.int32, shape, len(shape) - 1), 6)


def _stack_heads(q, q4_ref, heads):
    tq = q.shape[0]
    head = _lane_head(q.shape)
    for h in range(heads):
        q4_ref[h * tq:(h + 1) * tq, :] = jnp.where(head == h, q, jnp.zeros_like(q)).astype(q4_ref.dtype)


def _unstack_heads(acc, heads):
    tq = acc.shape[0] // heads
    head = _lane_head((tq, acc.shape[1]))
    out = acc[0:tq]
    for h in range(1, heads):
        out = jnp.where(head == h, acc[h * tq:(h + 1) * tq], out)
    return out


def _sb_prompt_kernel(q_ref, k_ref, v_ref, u_ref, o_ref, q4_ref, carry_ref, acc_ref, p_ref):
    i = pl.program_id(1)
    tq = TQ_ATT
    _stack_heads(q_ref[...], q4_ref, H_SB)
    carry_ref[...] = jnp.zeros(carry_ref.shape, F32)
    acc_ref[...] = jnp.zeros(acc_ref.shape, F32)
    qpos = i * tq + lax.broadcasted_iota(jnp.int32, (tq, 1), 0)

    def tile(j, c):
        k0 = pl.multiple_of((i - j) * tq, tq)
        z_all = _dot_nt(q4_ref[...], k_ref[pl.ds(k0, tq), :])
        mask = (k0 + lax.broadcasted_iota(jnp.int32, (tq, tq), 1)) < qpos
        for h in range(H_SB):
            rows = slice(h * tq, (h + 1) * tq)
            z = z_all[rows]
            sp = jnp.log(1.0 + jnp.exp(-jnp.abs(z)))
            lk = jnp.where(mask, -(jnp.maximum(z, 0.0) + sp), 0.0)
            after = _dot_f32(lk, u_ref[...])
            car = carry_ref[rows]
            a = jnp.where(mask, jnp.exp(jnp.minimum(z, 0.0) - sp + after + car), 0.0)
            carry_ref[rows] = car + after[:, 0:1] + lk[:, 0:1]
            p_ref[rows, :] = a.astype(BF16)
        acc_ref[...] += _dot(p_ref[...], v_ref[pl.ds(k0, tq), :])
        return c

    lax.fori_loop(0, i + 1, tile, 0)
    o_ref[...] = _unstack_heads(acc_ref[...], H_SB)


def _sb_prompt(q, kvb, upper, batch, seq):
    nq = seq // TQ_ATT
    rows = H_SB * TQ_ATT
    qmap = lambda b, i: (b * nq + i, 0)
    return pl.pallas_call(
        _sb_prompt_kernel,
        grid=(batch, nq),
        in_specs=[pl.BlockSpec((TQ_ATT, 256), qmap),
                  pl.BlockSpec((seq, 256), lambda b, i: (b, 0)),
                  pl.BlockSpec((seq, 256), lambda b, i: (b, 1)),
                  pl.BlockSpec(upper.shape, lambda b, i: (0, 0))],
        out_specs=pl.BlockSpec((TQ_ATT, 256), qmap),
        out_shape=jax.ShapeDtypeStruct((batch * seq, 256), F32),
        scratch_shapes=[pltpu.VMEM((rows, 256), BF16), pltpu.VMEM((rows, 1), F32),
                        pltpu.VMEM((rows, 256), F32), pltpu.VMEM((rows, TQ_ATT), BF16)],
        compiler_params=_cparams(("parallel", "arbitrary")),
        name="sb_prompt",
    )(q, kvb, kvb, upper)


def _fox_prompt_kernel(q_ref, k_ref, v_ref, cq_ref, ck_ref, o_ref, q4_ref, m_ref, l_ref, acc_ref, p_ref):
    i = pl.program_id(1)
    tq = TQ_ATT
    _stack_heads(q_ref[...], q4_ref, H_FOX)
    _softmax_reset(m_ref, l_ref, acc_ref)
    qpos = i * tq + lax.broadcasted_iota(jnp.int32, (tq, 1), 0)
    cq = cq_ref[...]

    def tile(kt, c):
        k0 = pl.multiple_of(kt * tq, tq)
        s_all = _dot_nt(q4_ref[...], k_ref[pl.ds(k0, tq), :])
        mask = (k0 + lax.broadcasted_iota(jnp.int32, (tq, tq), 1)) <= qpos
        ck = ck_ref[0, kt]
        s = [s_all[h * tq:(h + 1) * tq] + (cq[:, GATE_COLS + h:GATE_COLS + h + 1] - ck[h:h + 1, :])
             for h in range(H_FOX)]
        _softmax_step(s, [mask] * H_FOX, v_ref[pl.ds(k0, tq), :], m_ref, l_ref, acc_ref, p_ref, H_FOX, tq)
        return c

    lax.fori_loop(0, i + 1, tile, 0)
    o_ref[...] = _unstack_heads(_softmax_finish(m_ref, l_ref, acc_ref), H_FOX)


def _fox_prompt(q, kvb, c_rows, c_lanes, batch, seq):
    nq = seq // TQ_ATT
    rows = H_FOX * TQ_ATT
    qmap = lambda b, i: (b * nq + i, 0)
    return pl.pallas_call(
        _fox_prompt_kernel,
        grid=(batch, nq),
        in_specs=[pl.BlockSpec((TQ_ATT, 256), qmap),
                  pl.BlockSpec((seq, 256), lambda b, i: (b, 0)),
                  pl.BlockSpec((seq, 256), lambda b, i: (b, 1)),
                  pl.BlockSpec((TQ_ATT, LANES), qmap),
                  pl.BlockSpec((1, nq, 8, TQ_ATT), lambda b, i: (b, 0, 0, 0))],
        out_specs=pl.BlockSpec((TQ_ATT, 256), qmap),
        out_shape=jax.ShapeDtypeStruct((batch * seq, 256), F32),
        scratch_shapes=[pltpu.VMEM((rows, 256), BF16), pltpu.VMEM((rows, 1), F32), pltpu.VMEM((rows, 1), F32),
                        pltpu.VMEM((rows, 256), F32), pltpu.VMEM((rows, TQ_ATT), BF16)],
        compiler_params=_cparams(("parallel", "arbitrary")),
        name="fox_prompt",
    )(q, kvb, kvb, c_rows, c_lanes)


CUMSUM_CHUNK = 512


def _cumsum_kernel(x_ref, tri_ref, o_ref):
    chunk = tri_ref.shape[0]
    carry = jnp.zeros((1, LANES), F32)
    for c in range(x_ref.shape[0] // chunk):
        rows = slice(c * chunk, (c + 1) * chunk)
        y = _dot_f32_left(tri_ref[...], x_ref[rows, :]) + carry
        o_ref[rows, :] = y
        carry = y[chunk - 1:chunk, :]


def _cumsum_rows(x, tri, batch, seq):
    return pl.pallas_call(
        _cumsum_kernel,
        grid=(batch,),
        in_specs=[pl.BlockSpec((seq, LANES), lambda b: (b, 0)), pl.BlockSpec(tri.shape, lambda b: (0, 0))],
        out_specs=pl.BlockSpec((seq, LANES), lambda b: (b, 0)),
        out_shape=jax.ShapeDtypeStruct(x.shape, F32),
        compiler_params=_cparams(("parallel",)),
        name="cumsum_rows",
    )(x, tri)


def _out_proj_kernel(on_ref, osb_ref, ofox_ref, g_ref, w_ref, x_ref, seg_ref, y_ref):
    seg = seg_ref[...]
    acc = x_ref[...]
    for ref, c0, width in ((on_ref, 0, 512), (osb_ref, 512, 256), (ofox_ref, 768, 256)):
        for j in range(width // 256):
            c = c0 + 256 * j
            o = _head_rms(ref[:, 256 * j:256 * (j + 1)], seg, g_ref[:, c:c + 256])
            acc = acc + _dot(o.astype(BF16), w_ref[c:c + 256, :])
    y_ref[...] = acc


def _out_proj(o_nsa, o_sb, o_fox, g, w, x, seg, tm):
    n = x.shape[0]
    row = lambda width: pl.BlockSpec((tm, width), lambda i: (i, 0))
    const = lambda a: pl.BlockSpec(a.shape, lambda i: (0, 0))
    return pl.pallas_call(
        _out_proj_kernel,
        grid=(n // tm,),
        in_specs=[row(512), row(256), row(256), const(g), const(w), row(D_MODEL), const(seg)],
        out_specs=row(D_MODEL),
        out_shape=jax.ShapeDtypeStruct(x.shape, F32),
        compiler_params=_cparams(("parallel",)),
        name="out_proj",
    )(o_nsa, o_sb, o_fox, g, w, x, seg)


def _mem_kv_kernel(x_ref, w_ref, gk_ref, seg_ref, kv_ref, kvb_ref):
    kv = _dot(x_ref[...].astype(BF16), w_ref[...])
    k = _head_rms(kv[:, :256], seg_ref[...], gk_ref[...])
    kv_ref[:, :256] = k
    kv_ref[:, 256:] = kv[:, 256:]
    kvb_ref[:, :256] = k.astype(BF16)
    kvb_ref[:, 256:] = kv[:, 256:].astype(BF16)


def _mem_kv(mem, w, gk, seg, tm):
    n = mem.shape[0]
    row = lambda width: pl.BlockSpec((tm, width), lambda i: (i, 0))
    const = lambda a: pl.BlockSpec(a.shape, lambda i: (0, 0))
    return pl.pallas_call(
        _mem_kv_kernel,
        grid=(n // tm,),
        in_specs=[row(D_MODEL), const(w), const(gk), const(seg)],
        out_specs=[row(512), row(512)],
        out_shape=[jax.ShapeDtypeStruct((n, 512), F32), jax.ShapeDtypeStruct((n, 512), BF16)],
        compiler_params=_cparams(("parallel",)),
        name="mem_kv",
    )(mem, w, gk, seg)


def _mem_q_kernel(x_ref, gn_ref, w_ref, gq_ref, seg_ref, q_ref):
    xn = _row_rms(x_ref[...], gn_ref[...]).astype(BF16)
    q = _dot(xn, w_ref[...])
    q_ref[...] = (_head_rms(q, seg_ref[...], gq_ref[...]) * SCALE).astype(q_ref.dtype)


def _mem_q(x, gn, w, gq, seg, tm, dtype):
    n = x.shape[0]
    row = lambda width: pl.BlockSpec((tm, width), lambda i: (i, 0))
    const = lambda a: pl.BlockSpec(a.shape, lambda i: (0, 0))
    return pl.pallas_call(
        _mem_q_kernel,
        grid=(n // tm,),
        in_specs=[row(D_MODEL), const(gn), const(w), const(gq), const(seg)],
        out_specs=row(256),
        out_shape=jax.ShapeDtypeStruct((n, 256), dtype),
        compiler_params=_cparams(("parallel",)),
        name="mem_q",
    )(x, gn, w, gq, seg)


def _mem_attn_kernel(q_ref, kv_ref, w_ref, x_ref, y_ref, q4_ref):
    _stack_heads(q_ref[0], q4_ref, H_MEM)
    s = _dot_nt(q4_ref[...].astype(BF16), kv_ref[0, 0, :, :256].astype(BF16))
    p = jnp.exp(s - jnp.max(s, axis=1, keepdims=True))
    p = p / jnp.sum(p, axis=1, keepdims=True)
    o = _unstack_heads(_dot(p.astype(BF16), kv_ref[0, 0, :, 256:].astype(BF16)), H_MEM)
    y_ref[0] = x_ref[0] + _dot(o.astype(BF16), w_ref[...])


def _mem_attn(q, kv, layer, w, x, tq):
    batch, seq, _ = x.shape
    return pl.pallas_call(
        _mem_attn_kernel,
        grid=(batch, seq // tq),
        in_specs=[pl.BlockSpec((1, tq, 256), lambda b, i: (b, i, 0)),
                  pl.BlockSpec((1, 1, N_MEM, 512), lambda b, i: (layer, b, 0, 0)),
                  pl.BlockSpec(w.shape, lambda b, i: (0, 0)),
                  pl.BlockSpec((1, tq, D_MODEL), lambda b, i: (b, i, 0))],
        out_specs=pl.BlockSpec((1, tq, D_MODEL), lambda b, i: (b, i, 0)),
        out_shape=jax.ShapeDtypeStruct(x.shape, F32),
        scratch_shapes=[pltpu.VMEM((H_MEM * tq, 256), q.dtype)],
        compiler_params=_cparams(("parallel", "parallel")),
        name="mem_attn",
    )(q, kv, w, x)


FF_CHUNK = 1024


def _mlp_kernel(x_ref, g_ref, wu_ref, wd_ref, y_ref, xn_ref):
    k = pl.program_id(1)

    @pl.when(k == 0)
    def _():
        x = x_ref[...]
        xn_ref[...] = _row_rms(x, g_ref[...]).astype(BF16)
        y_ref[...] = x

    h = jnp.maximum(_dot(xn_ref[...], wu_ref[...]), 0.0)
    y_ref[...] += _dot((h * h).astype(BF16), wd_ref[...])


def _mlp(x, g, w_up, w_down, tm):
    n = x.shape[0]
    return pl.pallas_call(
        _mlp_kernel,
        grid=(n // tm, D_FF // FF_CHUNK),
        in_specs=[pl.BlockSpec((tm, D_MODEL), lambda i, k: (i, 0)),
                  pl.BlockSpec(g.shape, lambda i, k: (0, 0)),
                  pl.BlockSpec((D_MODEL, FF_CHUNK), lambda i, k: (0, k)),
                  pl.BlockSpec((FF_CHUNK, D_MODEL), lambda i, k: (k, 0))],
        out_specs=pl.BlockSpec((tm, D_MODEL), lambda i, k: (i, 0)),
        out_shape=jax.ShapeDtypeStruct(x.shape, F32),
        scratch_shapes=[pltpu.VMEM((tm, D_MODEL), BF16)],
        compiler_params=_cparams(("parallel", "arbitrary")),
        name="mlp",
    )(x, g, w_up, w_down)


def _dot_f32x3(x, m):
    hi, mid, lo = _split3(x)
    return _dot(hi, m) + _dot(mid, m) + _dot(lo, m)


def _head_rows(q_row, rows=8):
    shape = (rows, q_row.shape[1])
    row = lax.broadcasted_iota(jnp.int32, shape, 0)
    return jnp.where(_lane_head(shape) == row, jnp.broadcast_to(q_row, shape), 0.0)


def _unstack_rows(acc, heads):
    head = _lane_head((1, acc.shape[1]))
    out = acc[0:1]
    for h in range(1, heads):
        out = jnp.where(head == h, acc[h:h + 1], out)
    return out


def _decode_pages_kernel(pt_ref, cmp_ref, sb_ref, fox_ref, lf_ref, qsb_ref, qfox_ref, foxnew_ref, lfnew_ref,
                         wcmp_ref, u_ref, kc_ref, osb_ref, ofox_ref,
                         qs8_ref, qf8_ref, car_ref, accs_ref, m_ref, l_ref, accf_ref, carf_ref):
    del pt_ref
    p = pl.program_id(1)
    last = pl.num_programs(1) - 1
    page = last - p

    @pl.when(p == 0)
    def _():
        qs8_ref[...] = _head_rows(qsb_ref[0])
        qf8 = _head_rows(qfox_ref[0])
        qf8_ref[...] = qf8
        car_ref[...] = jnp.zeros(car_ref.shape, F32)
        accs_ref[...] = jnp.zeros(accs_ref.shape, F32)
        new = foxnew_ref[0]
        m_ref[...] = jnp.sum(qf8 * new[:, :256], axis=1, keepdims=True)
        l_ref[...] = jnp.ones(l_ref.shape, F32)
        accf_ref[...] = jnp.broadcast_to(new[:, 256:], accf_ref.shape)
        carf_ref[...] = lfnew_ref[0][:, 0:1]

    u = u_ref[...]
    z = _dot_nt(qs8_ref[...].astype(BF16), sb_ref[0, 0, :, :256].astype(BF16))
    sp = jnp.log(1.0 + jnp.exp(-jnp.abs(z)))
    lk = -(jnp.maximum(z, 0.0) + sp)
    after = _dot_f32(lk, u)
    car = car_ref[...]
    a = jnp.exp(jnp.minimum(z, 0.0) - sp + after + car)
    car_ref[...] = car + after[:, 0:1] + lk[:, 0:1]
    accs_ref[...] += _dot(a.astype(BF16), sb_ref[0, 0, :, 256:].astype(BF16))

    lf = lf_ref[0, 0]
    suffix = _dot_f32x3(lf, u)
    carf = carf_ref[...]
    s = _dot_nt(qf8_ref[...].astype(BF16), fox_ref[0, 0, :, :256].astype(BF16)) + (suffix + carf)
    carf_ref[...] = carf + suffix[:, 0:1] + lf[:, 0:1]
    m_old = m_ref[...]
    m_new = jnp.maximum(m_old, jnp.max(s, axis=1, keepdims=True))
    pe = jnp.exp(s - m_new)
    alpha = jnp.exp(m_old - m_new)
    l_ref[...] = alpha * l_ref[...] + jnp.sum(pe, axis=1, keepdims=True)
    accf_ref[...] = alpha * accf_ref[...] + _dot(pe.astype(BF16), fox_ref[0, 0, :, 256:].astype(BF16))
    m_ref[...] = m_new

    rows = cmp_ref.shape[2]
    x = cmp_ref[0, 0].reshape(rows // CMP_BLOCK, CMP_BLOCK, 4 * HEAD_DIM)
    kc_ref[0, page] = jnp.sum(x * wcmp_ref[...][None], axis=1)

    @pl.when(p == last)
    def _():
        osb_ref[0] = _unstack_rows(accs_ref[...], H_SB)
        ofox_ref[0] = _unstack_rows(accf_ref[...] / l_ref[...], H_FOX)


def _decode_pages(pt_flat, layer, cmp_pool, sb_pool, fox_pool, lf_pool, qsb, qfox, foxnew, lfnew, wcmp, upper,
                  nseq, npages):
    page = cmp_pool.shape[2]
    per_page = page // CMP_BLOCK
    pmap = lambda b, p, pt: (layer, pt[jnp.minimum(b, nseq - 1) * npages + (npages - 1 - p)], 0, 0)
    smap = lambda b, p, pt: (b, 0, 0)
    cmap = lambda b, p, pt: (0, 0)
    grid_spec = pltpu.PrefetchScalarGridSpec(
        num_scalar_prefetch=1,
        grid=(nseq, npages),
        in_specs=[pl.BlockSpec((1, 1, page, 256), pmap),
                  pl.BlockSpec((1, 1, page, 512), pmap),
                  pl.BlockSpec((1, 1, page, 512), pmap),
                  pl.BlockSpec((1, 1, 8, page), pmap),
                  pl.BlockSpec((1, 1, 256), smap),
                  pl.BlockSpec((1, 1, 256), smap),
                  pl.BlockSpec((1, 1, 512), smap),
                  pl.BlockSpec((1, 8, LANES), smap),
                  pl.BlockSpec(wcmp.shape, cmap),
                  pl.BlockSpec(upper.shape, cmap)],
        out_specs=[pl.BlockSpec((1, npages, per_page, 256), lambda b, p, pt: (b, 0, 0, 0)),
                   pl.BlockSpec((1, 1, 256), smap),
                   pl.BlockSpec((1, 1, 256), smap)],
        scratch_shapes=[pltpu.VMEM((8, 256), F32), pltpu.VMEM((8, 256), F32), pltpu.VMEM((8, 1), F32),
                        pltpu.VMEM((8, 256), F32), pltpu.VMEM((8, 1), F32), pltpu.VMEM((8, 1), F32),
                        pltpu.VMEM((8, 256), F32), pltpu.VMEM((8, 1), F32)],
    )
    return pl.pallas_call(
        _decode_pages_kernel,
        grid_spec=grid_spec,
        out_shape=[jax.ShapeDtypeStruct((nseq, npages, per_page, 256), F32),
                   jax.ShapeDtypeStruct((nseq, 1, 256), F32),
                   jax.ShapeDtypeStruct((nseq, 1, 256), F32)],
        compiler_params=_cparams(("parallel", "arbitrary")),
        name="decode_pages",
    )(pt_flat, cmp_pool, sb_pool, fox_pool, lf_pool, qsb, qfox, foxnew, lfnew, wcmp, upper)


def _nsa_q8(q_row):
    low = _lane_lt64((1, LANES))
    rows = []
    for g in range(KVH_NSA):
        for r in range(NSA_REP):
            t = q_row[:, r * LANES:(r + 1) * LANES]
            rows.append(jnp.where(low if g == 0 else jnp.logical_not(low), t, 0.0))
    return jnp.concatenate(rows, axis=0)


def _slope_col():
    row = lax.broadcasted_iota(jnp.int32, (H_NSA, 1), 0)
    out = jnp.zeros((H_NSA, 1), F32)
    for h in range(H_NSA):
        out = jnp.where(row == h, _nsa_slope(h), out)
    return out


def _decode_cmp_kernel(kc_ref, q_ref, gk_ref, seg_ref, pair_ref, oc_ref, idx_ref, *, past_len):
    kcr = kc_ref[0]
    nbc = kcr.shape[0]
    k = _head_rms(kcr[:, :LANES], seg_ref[...], gk_ref[...]).astype(BF16)
    q8 = _nsa_q8(q_ref[0])
    slope = _slope_col()
    blk = lax.broadcasted_iota(jnp.int32, (1, nbc), 1)
    dc = past_len - (blk * CMP_BLOCK + (CMP_BLOCK - 1))
    vis = dc >= 0
    s = jnp.where(vis, _dot_nt(q8.astype(BF16), k) - slope * dc.astype(F32), NEG)
    m = jnp.max(s, axis=1, keepdims=True)
    p = jnp.where(vis, jnp.exp(s - m), 0.0)
    p = p / jnp.maximum(jnp.sum(p, axis=1, keepdims=True), 1e-30)
    oc_ref[0] = _dot(p.astype(BF16), kcr[:, LANES:].astype(BF16))
    row = lax.broadcasted_iota(jnp.int32, (8, nbc), 0)
    imp_c = jnp.zeros((8, nbc), F32)
    for g in range(KVH_NSA):
        tot = jnp.sum(p[g * NSA_REP:(g + 1) * NSA_REP], axis=0, keepdims=True)
        imp_c = jnp.where(row == g, jnp.broadcast_to(tot, (8, nbc)), imp_c)
    imp = _dot_f32(imp_c, pair_ref[...])
    lane = lax.broadcasted_iota(jnp.int32, (8, LANES), 1)
    cur = past_len // SEL_BLOCK
    nbs = past_len // SEL_BLOCK
    forced = (lane == 0) | (lane == cur - 1)
    score = jnp.where(lane < nbs, imp + jnp.where(forced, FORCE_BONUS, 0.0), NEG)
    lanef = lane.astype(F32)
    picks = jnp.zeros((8, LANES), F32)
    for it in range(SEL_TOPK - 1):
        best = jnp.max(score, axis=1, keepdims=True)
        first = jnp.min(jnp.where(score == best, lanef, 1e9), axis=1, keepdims=True)
        picks = jnp.where(lane == it, first, picks)
        score = jnp.where(lanef == first, NEG, score)
    idx_ref[0] = picks.astype(jnp.int32)


def _decode_cmp(kc_raw, q, gk, seg1, pair, past_len):
    nseq, nbc, _ = kc_raw.shape
    smap = lambda b: (b, 0, 0)
    const = lambda a: pl.BlockSpec(a.shape, lambda b: (0, 0))
    return pl.pallas_call(
        functools.partial(_decode_cmp_kernel, past_len=past_len),
        grid=(nseq,),
        in_specs=[pl.BlockSpec((1, nbc, 256), smap), pl.BlockSpec((1, 1, 512), smap),
                  const(gk), const(seg1), const(pair)],
        out_specs=[pl.BlockSpec((1, 8, LANES), smap), pl.BlockSpec((1, 8, LANES), smap)],
        out_shape=[jax.ShapeDtypeStruct((nseq, 8, LANES), F32), jax.ShapeDtypeStruct((nseq, 8, LANES), jnp.int32)],
        compiler_params=_cparams(("parallel",)),
        name="decode_cmp",
    )(kc_raw, q, gk, seg1, pair)


N_PICK = SEL_TOPK - 1


def _decode_nsa_kernel(pt_ref, idx_ref, *refs, past_len):
    del pt_ref
    nblk = KVH_NSA * N_PICK
    blocks = refs[:nblk]
    q_ref, misc_ref, oc_ref, selnew_ref, win_ref, winnew_ref, o_ref, nwin_ref, kall_ref, vall_ref = refs[nblk:]
    b = pl.program_id(0)
    q8 = _nsa_q8(q_ref[0])
    q8b = q8.astype(BF16)
    slope = _slope_col()
    row8 = lax.broadcasted_iota(jnp.int32, (8, 1), 0)
    nkeys = SEL_TOPK * SEL_BLOCK
    lane = lax.broadcasted_iota(jnp.int32, (1, nkeys), 1)
    slot = jnp.right_shift(lane, 6)
    r64 = lax.broadcasted_iota(jnp.int32, (SEL_BLOCK, 1), 0)
    new = selnew_ref[0]

    o_sel = jnp.zeros((8, LANES), F32)
    for g in range(KVH_NSA):
        blk_of_lane = jnp.full((1, nkeys), past_len // SEL_BLOCK, jnp.int32)
        for j in range(N_PICK):
            blkv = blocks[g * N_PICK + j][0, 0]
            kall_ref[j * SEL_BLOCK:(j + 1) * SEL_BLOCK, :] = blkv[:, :LANES].astype(BF16)
            vall_ref[j * SEL_BLOCK:(j + 1) * SEL_BLOCK, :] = blkv[:, LANES:].astype(BF16)
            blk_of_lane = jnp.where(slot == j, idx_ref[(b * 8 + g) * LANES + j], blk_of_lane)
        base = N_PICK * SEL_BLOCK
        kall_ref[base:base + SEL_BLOCK, :] = jnp.where(r64 == 0, jnp.broadcast_to(new[:, :LANES], (SEL_BLOCK, LANES)),
                                                       0.0).astype(BF16)
        vall_ref[base:base + SEL_BLOCK, :] = jnp.where(r64 == 0, jnp.broadcast_to(new[:, LANES:], (SEL_BLOCK, LANES)),
                                                       0.0).astype(BF16)
        dist = past_len - (blk_of_lane * SEL_BLOCK + jnp.bitwise_and(lane, SEL_BLOCK - 1))
        ok = dist >= 0
        s = jnp.where(ok, _dot_nt(q8b, kall_ref[...]) - slope * dist.astype(F32), NEG)
        m = jnp.max(s, axis=1, keepdims=True)
        p = jnp.where(ok, jnp.exp(s - m), 0.0)
        og = _dot(p.astype(BF16), vall_ref[...]) / jnp.maximum(jnp.sum(p, axis=1, keepdims=True), 1e-30)
        o_sel = jnp.where(jnp.right_shift(row8, 2) == g, og, o_sel)

    kw = win_ref[0, 0]
    nbuf = kw.shape[0]
    wlane = lax.broadcasted_iota(jnp.int32, (1, nbuf), 1)
    dw = nbuf - wlane
    okw = (dw <= WINDOW) & (past_len - dw >= 0)
    wnew = winnew_ref[0]
    s = jnp.where(okw, _dot_nt(q8b, kw[:, :LANES].astype(BF16)) - slope * dw.astype(F32), NEG)
    s_new = jnp.sum(q8 * wnew[:, :LANES], axis=1, keepdims=True)
    m = jnp.maximum(jnp.max(s, axis=1, keepdims=True), s_new)
    p = jnp.where(okw, jnp.exp(s - m), 0.0)
    p_new = jnp.exp(s_new - m)
    o_win = (_dot(p.astype(BF16), kw[:, LANES:].astype(BF16)) + p_new * wnew[:, LANES:]) / (
        jnp.sum(p, axis=1, keepdims=True) + p_new)
    rows = lax.broadcasted_iota(jnp.int32, (nbuf, 1), 0)
    nwin_ref[0] = jnp.where(rows == nbuf - 1, jnp.broadcast_to(wnew, kw.shape), pltpu.roll(kw, nbuf - 1, 0))

    mrow = jnp.broadcast_to(misc_ref[0], (8, LANES))
    glane = lax.broadcasted_iota(jnp.int32, (8, LANES), 1)
    gate = [jnp.sum(jnp.where(glane == 3 * row8 + br, mrow, 0.0), axis=1, keepdims=True) for br in range(3)]
    out = gate[0] * oc_ref[0] + gate[1] * o_sel + gate[2] * o_win
    low = _lane_lt64((1, LANES))
    for r in range(NSA_REP):
        o_ref[0, :, r * LANES:(r + 1) * LANES] = jnp.where(low, out[r:r + 1], out[NSA_REP + r:NSA_REP + r + 1])


def _decode_nsa(pt_flat, idx_flat, layer, sel_pool, q, misc, oc, selnew, win_buf, winnew, npages, past_len):
    nseq = q.shape[0]
    nbuf = win_buf.shape[2]
    depth, n_pool, page, _ = sel_pool.shape
    per_page = page // SEL_BLOCK
    half_pool = sel_pool.reshape(depth, n_pool * per_page, SEL_BLOCK, 256)
    smap = lambda b, pt, idx: (b, 0, 0)

    def pick(g, j):
        def index_map(b, pt, idx):
            bb = jnp.minimum(b, nseq - 1)
            blk = jnp.clip(idx[(bb * 8 + g) * LANES + j], 0, npages * per_page - 1)
            return (layer, pt[bb * npages + blk // per_page] * per_page + blk % per_page, 0, 0)
        return pl.BlockSpec((1, 1, SEL_BLOCK, 256), index_map)

    grid_spec = pltpu.PrefetchScalarGridSpec(
        num_scalar_prefetch=2,
        grid=(nseq,),
        in_specs=[pick(g, j) for g in range(KVH_NSA) for j in range(N_PICK)]
        + [pl.BlockSpec((1, 1, 512), smap), pl.BlockSpec((1, 1, LANES), smap), pl.BlockSpec((1, 8, LANES), smap),
           pl.BlockSpec((1, 1, 256), smap), pl.BlockSpec((1, 1, nbuf, 256), lambda b, pt, idx: (layer, b, 0, 0)),
           pl.BlockSpec((1, 1, 256), smap)],
        out_specs=[pl.BlockSpec((1, 1, 512), smap), pl.BlockSpec((1, nbuf, 256), smap)],
        scratch_shapes=[pltpu.VMEM((SEL_TOPK * SEL_BLOCK, LANES), BF16), pltpu.VMEM((SEL_TOPK * SEL_BLOCK, LANES), BF16)],
    )
    return pl.pallas_call(
        functools.partial(_decode_nsa_kernel, past_len=past_len),
        grid_spec=grid_spec,
        out_shape=[jax.ShapeDtypeStruct((nseq, 1, 512), F32), jax.ShapeDtypeStruct((nseq, nbuf, 256), F32)],
        compiler_params=_cparams(("parallel",)),
        name="decode_nsa",
    )(pt_flat, idx_flat, *([half_pool] * (KVH_NSA * N_PICK)), q, misc, oc, selnew, win_buf, winnew)


NSA_HEAD_ORDER = (0, 4, 1, 5, 2, 6, 3, 7)


def _layer_params(l, g_mix_norm, w_in, b_nsa_gate, b_forget, g_qk, w_cmp, g_head_out, w_out, g_mem_norm,
                  w_mem_q, w_mem_kv, g_mem_qk, w_mem_o, g_mlp_norm, w_up, w_down):
    order = jnp.asarray(NSA_HEAD_ORDER)
    wi = w_in[l]
    wq = wi[:, :512].reshape(D_MODEL, H_NSA, HEAD_DIM)[:, order].reshape(D_MODEL, 512)
    pad = jnp.zeros((D_MODEL, LANES - GATE_COLS - H_FOX), F32)
    w_in_p = jnp.concatenate([wq, wi[:, 512:1280], wi[:, 1304:2840], wi[:, 1280:1304], wi[:, 2840:2844], pad],
                             axis=1).astype(BF16)
    bias = jnp.concatenate([b_nsa_gate[l], b_forget[l], jnp.zeros((LANES - GATE_COLS - H_FOX,), F32)])[None]
    gqk = g_qk[l]
    tile = lambda v, n: jnp.tile(v, n)[None]
    head_order = jnp.concatenate([order, jnp.arange(H_NSA, H_NSA + H_SB + H_FOX)])
    return dict(
        g_mix=g_mix_norm[l][None], w_in=w_in_p, bias=bias,
        gq=tile(gqk[0], 4), gsel=tile(gqk[2], 2), gwin=tile(gqk[3], 2), gfq=tile(gqk[4], 4), gfk=tile(gqk[5], 4),
        w_cmp=jnp.repeat(jnp.transpose(w_cmp[l], (1, 0, 2)).reshape(CMP_BLOCK, 4), HEAD_DIM, axis=1),
        gk_cmp=tile(gqk[1], 2),
        g_head=g_head_out[l].reshape(-1, HEAD_DIM)[head_order].reshape(1, -1),
        w_out=w_out[l].reshape(-1, HEAD_DIM, D_MODEL)[head_order].reshape(-1, D_MODEL).astype(BF16),
        g_mem=g_mem_norm[l][None], w_mem_q=w_mem_q[l].astype(BF16), w_mem_kv=w_mem_kv[l].astype(BF16),
        gq_mem=tile(g_mem_qk[l, 0], 4), gk_mem=tile(g_mem_qk[l, 1], 4), w_mem_o=w_mem_o[l].astype(BF16),
        g_mlp=g_mlp_norm[l][None], w_up=w_up[l].astype(BF16), w_down=w_down[l].astype(BF16),
    )


def _prompt_consts(seq):
    nbc = seq // CMP_BLOCK
    n = jnp.arange(nbc)
    pair = (n[:, None] // 2 == jnp.arange(LANES)[None, :]).astype(BF16)
    kt = jnp.arange(seq // TK_SEL)[:, None, None]
    j = jnp.arange(LANES)[None, :, None]
    c = jnp.arange(TK_SEL)[None, None, :]
    expand = (j == kt * (TK_SEL // SEL_BLOCK) + c // SEL_BLOCK).astype(BF16)
    return dict(seg=_seg_matrix(256), pair=pair, expand=expand, upper=_strict_upper(TQ_ATT),
                tri=_lower_incl(CUMSUM_CHUNK))


def _prompt_layer(x, mem_rows, p, c, batch, seq):
    seg = c["seg"]
    (q, cmp, sel, selb, win, winb, misc, qsb, sb, sbb, qfox, fox, foxb) = _in_proj(
        x, p["g_mix"], p["w_in"], p["bias"], p["gq"], p["gsel"], p["gwin"], p["gfq"], p["gfk"], seg, 512)
    kc, vc = _compress(cmp, p["w_cmp"], p["gk_cmp"], seg[:LANES, :LANES], 1024)
    nbc = seq // CMP_BLOCK
    o_nsa = _nsa_prompt(q, misc, kc.reshape(batch, nbc, LANES), vc.reshape(batch, nbc, LANES), selb, winb,
                        c["pair"], c["expand"], batch, seq)
    o_sb = _sb_prompt(qsb, sbb, c["upper"], batch, seq)
    c_rows = _cumsum_rows(misc, c["tri"], batch, seq)
    nq = seq // TQ_ATT
    c_lanes = jnp.transpose(c_rows[:, GATE_COLS:GATE_COLS + 8].reshape(batch, nq, TQ_ATT, 8), (0, 1, 3, 2))
    o_fox = _fox_prompt(qfox, foxb, c_rows, c_lanes, batch, seq)
    x = _out_proj(o_nsa, o_sb, o_fox, p["g_head"], p["w_out"], x, seg, 512)
    kvm, kvmb = _mem_kv(mem_rows, p["w_mem_kv"], p["gk_mem"], seg, 256)
    qm = _mem_q(x, p["g_mem"], p["w_mem_q"], p["gq_mem"], seg, 512, BF16)
    x = _mem_attn(qm.reshape(batch, seq, 256), kvmb.reshape(1, batch, N_MEM, 512), 0, p["w_mem_o"],
                  x.reshape(batch, seq, D_MODEL), 512).reshape(batch * seq, D_MODEL)
    x = _mlp(x, p["g_mlp"], p["w_up"], p["w_down"], 1024)
    kv5 = lambda a, h: a.reshape(batch, -1, 2, h, HEAD_DIM)
    w_keep = min(WINDOW, seq)
    state = (kv5(cmp, KVH_NSA), kv5(sel, KVH_NSA), kv5(win, KVH_NSA)[:, seq - w_keep:], kv5(sb, H_SB),
             kv5(fox, H_FOX), misc[:, GATE_COLS:GATE_COLS + H_FOX].reshape(batch, seq, H_FOX), kv5(kvm, H_MEM))
    return x, state


def _sample_layer(x, pools, layer, p, c, pt_flat, npages, past_len):
    cmp_pool, sel_pool, win_buf, sb_pool, fox_pool, lf_pool, mem_kv = pools
    nseq = x.shape[0]
    seg = c["seg"]
    seg1 = seg[:LANES, :LANES]
    (q, cmp, sel, _, win, _, misc, qsb, sb, _, qfox, fox, _) = _in_proj(
        x, p["g_mix"], p["w_in"], p["bias"], p["gq"], p["gsel"], p["gwin"], p["gfq"], p["gfk"], seg, nseq)
    r3 = lambda a: a.astype(F32).reshape(nseq, 1, -1)
    logf = misc[:, GATE_COLS:GATE_COLS + H_FOX]
    lfnew = jnp.broadcast_to(jnp.pad(logf, ((0, 0), (0, 8 - H_FOX)))[:, :, None], (nseq, 8, LANES))
    kc4, o_sb, o_fox = _decode_pages(pt_flat, layer, cmp_pool, sb_pool, fox_pool, lf_pool, r3(qsb), r3(qfox),
                                     r3(fox), lfnew, p["w_cmp"], c["upper_page"], nseq, npages)
    kc_raw = kc4.reshape(nseq, -1, 256)
    oc, idx = _decode_cmp(kc_raw, r3(q), p["gk_cmp"], seg1, c["pair_dec"], past_len)
    o_nsa, new_win = _decode_nsa(pt_flat, idx.reshape(-1), layer, sel_pool, r3(q), r3(misc), oc, r3(sel), win_buf,
                                 r3(win), npages, past_len)
    x = _out_proj(o_nsa.reshape(nseq, 512), o_sb.reshape(nseq, 256), o_fox.reshape(nseq, 256),
                  p["g_head"], p["w_out"], x, seg, nseq)
    qm = _mem_q(x, p["g_mem"], p["w_mem_q"], p["gq_mem"], seg, nseq, F32)
    rep = lambda a: jnp.broadcast_to(a[:, None, :], (nseq, 8, a.shape[-1]))
    x = _mem_attn(rep(qm), mem_kv, layer, p["w_mem_o"], rep(x), 8)[:, 0]
    x = _mlp(x, p["g_mlp"], p["w_up"], p["w_down"], nseq)
    kv5 = lambda a, h: a.reshape(nseq, -1, 2, h, HEAD_DIM)
    state = (kv5(cmp, KVH_NSA), kv5(sel, KVH_NSA), kv5(new_win, KVH_NSA), kv5(sb, H_SB), kv5(fox, H_FOX),
             logf.reshape(nseq, 1, H_FOX))
    return x, state


def _decode_consts(npages, page):
    nbc = npages * page // CMP_BLOCK
    pair = (jnp.arange(nbc)[:, None] // 2 == jnp.arange(LANES)[None, :]).astype(BF16)
    return dict(seg=_seg_matrix(256), upper_page=_strict_upper(page), pair_dec=pair)


def kernel(x_prompt, x_sample, mem_prompt, cache_nsa_cmp, cache_nsa_sel, cache_nsa_win, cache_sb, cache_fox,
           cache_fox_logf, cache_mem, page_table, g_mix_norm, w_in, b_nsa_gate, b_forget, g_qk, w_cmp,
           g_head_out, w_out, g_mem_norm, w_mem_q, w_mem_kv, g_mem_qk, w_mem_o, g_mlp_norm, w_up, w_down):
    batch, seq, _ = x_prompt.shape
    nseq, dec_seq, _ = x_sample.shape
    assert dec_seq == 1, "the decode kernels take one new token per sequence"
    depth, n_pool, page = cache_sb.shape[:3]
    npages = page_table.shape[1]
    past_len = npages * page
    assert past_len // SEL_BLOCK <= LANES and cache_nsa_win.shape[2] <= past_len
    pconsts = _prompt_consts(seq)
    dconsts = _decode_consts(npages, page)
    pt_flat = page_table.reshape(-1).astype(jnp.int32)
    cmp_pool = cache_nsa_cmp.reshape(depth, n_pool, page, 256)
    sel_pool = cache_nsa_sel.reshape(depth, n_pool, page, 256)
    win_buf = cache_nsa_win.reshape(depth, nseq, -1, 256)
    sb_pool = cache_sb.reshape(depth, n_pool, page, 512)
    fox_pool = cache_fox.reshape(depth, n_pool, page, 512)
    lf_pool = jnp.pad(jnp.transpose(cache_fox_logf, (0, 1, 3, 2)), ((0, 0), (0, 0), (0, 8 - H_FOX), (0, 0)))
    mem_kv = cache_mem.reshape(depth, nseq, N_MEM, 512)

    xp = x_prompt.reshape(batch * seq, D_MODEL)
    xs = x_sample.reshape(nseq, D_MODEL)
    mem_rows = mem_prompt.reshape(batch * N_MEM, D_MODEL)
    pools = (cmp_pool, sel_pool, win_buf, sb_pool, fox_pool, lf_pool, mem_kv)
    pstates, sstates = [], []
    for l in range(depth):
        p = _layer_params(l, g_mix_norm, w_in, b_nsa_gate, b_forget, g_qk, w_cmp, g_head_out, w_out, g_mem_norm,
                          w_mem_q, w_mem_kv, g_mem_qk, w_mem_o, g_mlp_norm, w_up, w_down)
        xp, st = _prompt_layer(xp, mem_rows, p, pconsts, batch, seq)
        pstates.append(st)
        xs, st = _sample_layer(xs, pools, l, p, dconsts, pt_flat, npages, past_len)
        sstates.append(st)
    stack = lambda states: tuple(jnp.stack([s[k] for s in states], axis=0) for k in range(len(states[0])))
    return (xp.reshape(batch, seq, D_MODEL), xs.reshape(nseq, 1, D_MODEL)) + stack(pstates) + stack(sstates)
```

```python
import functools

import jax
import jax.numpy as jnp
from jax import lax
from jax.experimental import pallas as pl
from jax.experimental.pallas import tpu as pltpu

F32 = jnp.float32
BF16 = jnp.bfloat16

D_MODEL = 1024
HEAD_DIM = 64
LANES = 128
H_NSA, KVH_NSA, NSA_REP = 8, 2, 4
H_SB, H_FOX, H_MEM = 4, 4, 4
CMP_BLOCK, SEL_BLOCK, SEL_TOPK, WINDOW = 32, 64, 16, 512
N_MEM = 256
D_FF = 4 * D_MODEL
EPS = 1e-6
FORCE_BONUS = 1e4
SCALE = HEAD_DIM ** -0.5
NEG = -1e30
VMEM_LIMIT = 48 * 1024 * 1024

C_Q, C_CMP, C_SEL, C_WIN, C_SB, C_FOX, C_MISC = 0, 512, 768, 1024, 1280, 2048, 2816
IN_PAD = 2944
GATE_COLS = 3 * H_NSA


def _cparams(sem):
    return pltpu.CompilerParams(dimension_semantics=sem, vmem_limit_bytes=VMEM_LIMIT)


def _dot(a, b):
    return jnp.dot(a, b, preferred_element_type=F32)


def _dot_nt(a, b):
    return lax.dot_general(a, b, (((1,), (1,)), ((), ())), preferred_element_type=F32)


def _split2(x):
    hi = x.astype(BF16)
    lo = (x - hi.astype(F32)).astype(BF16)
    return hi, lo


def _split3(x):
    hi = x.astype(BF16)
    r = x - hi.astype(F32)
    mid = r.astype(BF16)
    lo = (r - mid.astype(F32)).astype(BF16)
    return hi, mid, lo


def _dot_f32(x, m):
    hi, lo = _split2(x)
    return _dot(hi, m) + _dot(lo, m)


def _dot_f32_left(m, x):
    hi, mid, lo = _split3(x)
    return _dot(m, hi) + _dot(m, mid) + _dot(m, lo)


def _head_rms(y, seg, g):
    ss = _dot_f32(y * y, seg)
    return y * lax.rsqrt(ss * (1.0 / HEAD_DIM) + EPS) * g


def _row_rms(x, g):
    return x * lax.rsqrt(jnp.mean(x * x, axis=-1, keepdims=True) + EPS) * g


def _log_sigmoid(z):
    return jnp.minimum(z, 0.0) - jnp.log(1.0 + jnp.exp(-jnp.abs(z)))


def _sigmoid(z):
    return 1.0 / (1.0 + jnp.exp(-z))


def _lane_lt64(shape):
    return lax.broadcasted_iota(jnp.int32, shape, len(shape) - 1) < HEAD_DIM


def _seg_matrix(n):
    i = jnp.arange(n) // HEAD_DIM
    return (i[:, None] == i[None, :]).astype(BF16)


def _strict_upper(n):
    i = jnp.arange(n)
    return (i[:, None] > i[None, :]).astype(BF16)


def _lower_incl(n):
    i = jnp.arange(n)
    return (i[None, :] <= i[:, None]).astype(BF16)


def _in_proj_kernel(x_ref, gn_ref, w_ref, bias_ref, gq_ref, gsel_ref, gwin_ref, gfq_ref, gfk_ref, seg_ref,
                    q_ref, cmp_ref, sel_ref, selb_ref, win_ref, winb_ref, misc_ref,
                    qsb_ref, sb_ref, sbb_ref, qfox_ref, fox_ref, foxb_ref):
    xn = _row_rms(x_ref[...], gn_ref[...]).astype(BF16)
    seg = seg_ref[...]
    seg1 = seg[:LANES, :LANES]

    def proj(c0, width):
        return _dot(xn, w_ref[:, c0:c0 + width])

    for j in range(2):
        q = proj(C_Q + 256 * j, 256)
        q_ref[:, 256 * j:256 * (j + 1)] = (_head_rms(q, seg, gq_ref[...]) * SCALE).astype(BF16)

    cmp_ref[...] = proj(C_CMP, 256)

    for c0, g_ref, o_ref, ob_ref in ((C_SEL, gsel_ref, sel_ref, selb_ref), (C_WIN, gwin_ref, win_ref, winb_ref)):
        kv = proj(c0, 256)
        k = _head_rms(kv[:, :LANES], seg1, g_ref[...])
        o_ref[:, :LANES] = k
        o_ref[:, LANES:] = kv[:, LANES:]
        ob_ref[:, :LANES] = k.astype(BF16)
        ob_ref[:, LANES:] = kv[:, LANES:].astype(BF16)

    z = proj(C_MISC, LANES) + bias_ref[...]
    lane = lax.broadcasted_iota(jnp.int32, z.shape, 1)
    misc_ref[...] = jnp.where(lane < GATE_COLS, _sigmoid(z), _log_sigmoid(z))

    qkv = proj(C_SB, 768)
    qsb_ref[...] = (qkv[:, :256] * SCALE).astype(BF16)
    sb_ref[...] = qkv[:, 256:]
    sbb_ref[...] = qkv[:, 256:].astype(BF16)

    qkv = proj(C_FOX, 768)
    qfox_ref[...] = (_head_rms(qkv[:, :256], seg, gfq_ref[...]) * SCALE).astype(BF16)
    k = _head_rms(qkv[:, 256:512], seg, gfk_ref[...])
    fox_ref[:, :256] = k
    fox_ref[:, 256:] = qkv[:, 512:]
    foxb_ref[:, :256] = k.astype(BF16)
    foxb_ref[:, 256:] = qkv[:, 512:].astype(BF16)


def _in_proj(x, gn, w, bias, gq, gsel, gwin, gfq, gfk, seg, tm):
    n = x.shape[0]
    row = lambda width: pl.BlockSpec((tm, width), lambda i: (i, 0))
    const = lambda a: pl.BlockSpec(a.shape, lambda i: (0, 0))
    widths = ((512, BF16), (256, F32), (256, F32), (256, BF16), (256, F32), (256, BF16), (LANES, F32),
              (256, BF16), (512, F32), (512, BF16), (256, BF16), (512, F32), (512, BF16))
    consts = (gn, w, bias, gq, gsel, gwin, gfq, gfk, seg)
    return pl.pallas_call(
        _in_proj_kernel,
        grid=(n // tm,),
        in_specs=[row(D_MODEL)] + [const(a) for a in consts],
        out_specs=[row(wd) for wd, _ in widths],
        out_shape=[jax.ShapeDtypeStruct((n, wd), dt) for wd, dt in widths],
        compiler_params=_cparams(("parallel",)),
        name="in_proj",
    )(x, *consts)


def _compress_kernel(x_ref, w_ref, gk_ref, seg_ref, kc_ref, vc_ref):
    rows = x_ref.shape[0]
    x = x_ref[...].reshape(rows // CMP_BLOCK, CMP_BLOCK, 4 * HEAD_DIM)
    c = jnp.sum(x * w_ref[...][None], axis=1)
    kc_ref[...] = _head_rms(c[:, :LANES], seg_ref[...], gk_ref[...]).astype(BF16)
    vc_ref[...] = c[:, LANES:].astype(BF16)


def _compress(kv_cmp, w_rows, gk, seg1, tr):
    n = kv_cmp.shape[0]
    nb = tr // CMP_BLOCK
    out = jax.ShapeDtypeStruct((n // CMP_BLOCK, LANES), BF16)
    return pl.pallas_call(
        _compress_kernel,
        grid=(n // tr,),
        in_specs=[pl.BlockSpec((tr, 256), lambda i: (i, 0)),
                  pl.BlockSpec(w_rows.shape, lambda i: (0, 0)),
                  pl.BlockSpec(gk.shape, lambda i: (0, 0)),
                  pl.BlockSpec(seg1.shape, lambda i: (0, 0))],
        out_specs=[pl.BlockSpec((nb, LANES), lambda i: (i, 0))] * 2,
        out_shape=[out, out],
        compiler_params=_cparams(("parallel",)),
        name="compress",
    )(kv_cmp, w_rows, gk, seg1)


TQ_NSA = 128
TK_SEL = 512
SEL_OFF = 16384.0


def _flash_head(s, r0, m_ref, l_ref, acc_ref, p_ref):
    rows, tk = s.shape
    rs = slice(r0, r0 + rows)
    m_old = m_ref[rs]
    m_new = jnp.maximum(m_old, jnp.max(s, axis=1, keepdims=True))
    psum = jnp.zeros((rows, LANES), F32)
    for c in range(tk // LANES):
        cs = slice(c * LANES, (c + 1) * LANES)
        p = jnp.exp(s[:, cs] - m_new)
        p_ref[rs, cs] = p.astype(BF16)
        psum = psum + p
    alpha = jnp.exp(m_old - m_new)
    l_ref[rs] = alpha * l_ref[rs] + jnp.sum(psum, axis=1, keepdims=True)
    for c in range(acc_ref.shape[1] // LANES):
        cs = slice(c * LANES, (c + 1) * LANES)
        acc_ref[rs, cs] = alpha * acc_ref[rs, cs]
    m_ref[rs] = m_new


def _softmax_finish(m_ref, l_ref, acc_ref):
    inv = 1.0 / jnp.maximum(l_ref[...], 1e-30)
    return jnp.concatenate([acc_ref[:, c * LANES:(c + 1) * LANES] * inv for c in range(acc_ref.shape[1] // LANES)],
                           axis=1)


def _softmax_reset(m_ref, l_ref, acc_ref):
    m_ref[...] = jnp.full(m_ref.shape, NEG, F32)
    l_ref[...] = jnp.zeros(l_ref.shape, F32)
    acc_ref[...] = jnp.zeros(acc_ref.shape, F32)


def _nsa_slope(h):
    return 2.0 ** (-(h + 1))


def _nsa_prompt_kernel(q_ref, gate_ref, kc_ref, vc_ref, ks_ref, vs_ref, kw_ref, vw_ref, pair_ref, exp_ref,
                       o_ref, q8_ref, m_ref, l_ref, acc_ref, p_ref, oc_ref, seln_ref):
    i = pl.program_id(1)
    tq = TQ_NSA
    q0 = i * tq
    nbc = kc_ref.shape[1]
    nbs = nbc // 2
    low = _lane_lt64((tq, LANES))
    for g in range(KVH_NSA):
        for r in range(NSA_REP):
            qt = q_ref[:, r * LANES:(r + 1) * LANES]
            keep = low if g == 0 else jnp.logical_not(low)
            q8_ref[(g * NSA_REP + r) * tq:(g * NSA_REP + r + 1) * tq, :] = jnp.where(keep, qt, jnp.zeros_like(qt))
    qpos = q0 + lax.broadcasted_iota(jnp.int32, (tq, 1), 0)

    s_all = _dot_nt(q8_ref[...], kc_ref[0])
    blk = lax.broadcasted_iota(jnp.int32, (tq, nbc), 1)
    dc = qpos - (blk * CMP_BLOCK + (CMP_BLOCK - 1))
    vis = dc >= 0
    dcf = dc.astype(F32)
    lane = lax.broadcasted_iota(jnp.int32, (tq, LANES), 1)
    visible = (lane * SEL_BLOCK <= qpos) & (lane < nbs)
    scores = []
    for g in range(KVH_NSA):
        imp_c = jnp.zeros((tq, nbc), F32)
        for r in range(NSA_REP):
            h = g * NSA_REP + r
            s = jnp.where(vis, s_all[h * tq:(h + 1) * tq] - _nsa_slope(h) * dcf, NEG)
            m = jnp.max(s, axis=1, keepdims=True)
            p = jnp.where(vis, jnp.exp(s - m), 0.0)
            p = p / jnp.maximum(jnp.sum(p, axis=1, keepdims=True), 1e-30)
            imp_c = imp_c + p
            p_ref[h * tq:(h + 1) * tq, :nbc] = p.astype(BF16)
        imp = _dot_f32(imp_c, pair_ref[...])
        cur = jnp.right_shift(qpos, 6)
        forced = (lane == 0) | (lane == cur) | (lane == cur - 1)
        scores.append(jnp.where(visible, imp + jnp.where(forced, FORCE_BONUS, 0.0), NEG))
    oc_ref[...] = _dot(p_ref[:, :nbc], vc_ref[0])

    few = q0 + tq <= SEL_TOPK * SEL_BLOCK

    @pl.when(few)
    def _():
        keep = jnp.where(visible, 0.0, -SEL_OFF).astype(BF16)
        for g in range(KVH_NSA):
            seln_ref[g] = keep

    @pl.when(jnp.logical_not(few))
    def _():
        score = jnp.concatenate(scores, axis=0)
        lanef = jnp.concatenate([lane, lane], axis=0).astype(F32)
        sel = jnp.full(score.shape, -SEL_OFF, F32)
        for _ in range(min(SEL_TOPK, nbs)):
            best = jnp.max(score, axis=1, keepdims=True)
            first = jnp.min(jnp.where(score == best, lanef, 1e9), axis=1, keepdims=True)
            hit = lanef == first
            sel = jnp.where(hit, 0.0, sel)
            score = jnp.where(hit, NEG, score)
        for g in range(KVH_NSA):
            seln_ref[g] = sel[g * tq:(g + 1) * tq].astype(BF16)

    def branch_out():
        return _softmax_finish(m_ref, l_ref, acc_ref)

    _softmax_reset(m_ref, l_ref, acc_ref)

    def sel_tile(kt, carry):
        k0 = pl.multiple_of(kt * TK_SEL, TK_SEL)
        s_all = _dot_nt(q8_ref[...], ks_ref[pl.ds(k0, TK_SEL), :])
        dist = qpos - (k0 + lax.broadcasted_iota(jnp.int32, (tq, TK_SEL), 1))
        distf = dist.astype(F32)
        for g in range(KVH_NSA):
            bias = jnp.where(dist >= 0, _dot(seln_ref[g], exp_ref[kt]), NEG)
            for r in range(NSA_REP):
                h = g * NSA_REP + r
                _flash_head(s_all[h * tq:(h + 1) * tq] - _nsa_slope(h) * distf + bias, h * tq,
                            m_ref, l_ref, acc_ref, p_ref)
        acc_ref[...] += _dot(p_ref[...], vs_ref[pl.ds(k0, TK_SEL), :])
        return carry

    lax.fori_loop(0, (q0 + tq - 1) // TK_SEL + 1, sel_tile, 0)
    o_sel = branch_out()

    _softmax_reset(m_ref, l_ref, acc_ref)

    def win_tile(kt, carry):
        k0 = pl.multiple_of(kt * tq, tq)
        s_all = _dot_nt(q8_ref[...], kw_ref[pl.ds(k0, tq), :])
        dist = qpos - (k0 + lax.broadcasted_iota(jnp.int32, (tq, tq), 1))
        distf = dist.astype(F32)
        bias = jnp.where(jnp.where(dist >= 0, dist, WINDOW + 1) <= WINDOW, 0.0, NEG)
        for h in range(H_NSA):
            _flash_head(s_all[h * tq:(h + 1) * tq] - _nsa_slope(h) * distf + bias, h * tq,
                        m_ref, l_ref, acc_ref, p_ref)
        acc_ref[...] += _dot(p_ref[:, :tq], vw_ref[pl.ds(k0, tq), :])
        return carry

    lax.fori_loop(jnp.maximum(i - WINDOW // tq, 0), i + 1, win_tile, 0)
    o_win = branch_out()

    gates = gate_ref[...]
    oc = oc_ref[...]
    for r in range(NSA_REP):
        halves = []
        for g in range(KVH_NSA):
            h = g * NSA_REP + r
            rows = slice(h * tq, (h + 1) * tq)
            halves.append(gates[:, 3 * h:3 * h + 1] * oc[rows] + gates[:, 3 * h + 1:3 * h + 2] * o_sel[rows]
                          + gates[:, 3 * h + 2:3 * h + 3] * o_win[rows])
        o_ref[:, r * LANES:(r + 1) * LANES] = jnp.where(low, halves[0], halves[1])


def _nsa_prompt(q, misc, kc, vc, selb, winb, pair, expand, batch, seq):
    nq = seq // TQ_NSA
    nbc = seq // CMP_BLOCK
    rows = H_NSA * TQ_NSA
    qmap = lambda b, i: (b * nq + i, 0)
    return pl.pallas_call(
        _nsa_prompt_kernel,
        grid=(batch, nq),
        in_specs=[pl.BlockSpec((TQ_NSA, 512), qmap),
                  pl.BlockSpec((TQ_NSA, LANES), qmap),
                  pl.BlockSpec((1, nbc, LANES), lambda b, i: (b, 0, 0)),
                  pl.BlockSpec((1, nbc, LANES), lambda b, i: (b, 0, 0)),
                  pl.BlockSpec((seq, LANES), lambda b, i: (b, 0)),
                  pl.BlockSpec((seq, LANES), lambda b, i: (b, 1)),
                  pl.BlockSpec((seq, LANES), lambda b, i: (b, 0)),
                  pl.BlockSpec((seq, LANES), lambda b, i: (b, 1)),
                  pl.BlockSpec(pair.shape, lambda b, i: (0, 0)),
                  pl.BlockSpec(expand.shape, lambda b, i: (0, 0, 0))],
        out_specs=pl.BlockSpec((TQ_NSA, 512), qmap),
        out_shape=jax.ShapeDtypeStruct((batch * seq, 512), F32),
        scratch_shapes=[pltpu.VMEM((rows, LANES), BF16),
                        pltpu.VMEM((rows, LANES), F32), pltpu.VMEM((rows, LANES), F32),
                        pltpu.VMEM((rows, LANES), F32),
                        pltpu.VMEM((rows, TK_SEL), BF16),
                        pltpu.VMEM((rows, LANES), F32),
                        pltpu.VMEM((KVH_NSA, TQ_NSA, LANES), BF16)],
        compiler_params=_cparams(("parallel", "arbitrary")),
        name="nsa_prompt",
    )(q, misc, kc, vc, selb, selb, winb, winb, pair, expand)


TQ_ATT = 256


def _lane_head(shape):
    return jnp.right_shift(lax.broadcasted_iota(jnp.int32, shape, len(shape) - 1), 6)


def _stack_heads(q, q4_ref, heads):
    tq = q.shape[0]
    head = _lane_head(q.shape)
    for h in range(heads):
        q4_ref[h * tq:(h + 1) * tq, :] = jnp.where(head == h, q, jnp.zeros_like(q)).astype(q4_ref.dtype)


def _unstack_heads(acc, heads):
    tq = acc.shape[0] // heads
    head = _lane_head((tq, acc.shape[1]))
    out = acc[0:tq]
    for h in range(1, heads):
        out = jnp.where(head == h, acc[h * tq:(h + 1) * tq], out)
    return out


def _sb_prompt_kernel(q_ref, k_ref, v_ref, u_ref, o_ref, q4_ref, carry_ref, acc_ref, p_ref):
    i = pl.program_id(1)
    tq = TQ_ATT
    _stack_heads(q_ref[...], q4_ref, H_SB)
    carry_ref[...] = jnp.zeros(carry_ref.shape, F32)
    acc_ref[...] = jnp.zeros(acc_ref.shape, F32)
    qpos = i * tq + lax.broadcasted_iota(jnp.int32, (tq, 1), 0)

    def tile(j, c):
        k0 = pl.multiple_of((i - j) * tq, tq)
        z_all = _dot_nt(q4_ref[...], k_ref[pl.ds(k0, tq), :])
        mask = (k0 + lax.broadcasted_iota(jnp.int32, (tq, tq), 1)) < qpos
        for h in range(H_SB):
            rows = slice(h * tq, (h + 1) * tq)
            z = z_all[rows]
            sp = jnp.log(1.0 + jnp.exp(-jnp.abs(z)))
            lk = jnp.where(mask, -(jnp.maximum(z, 0.0) + sp), 0.0)
            e = jnp.minimum(z, 0.0) - sp + _dot_f32(lk, u_ref[...])
            car = carry_ref[rows]
            for cc in range(tq // LANES):
                cs = slice(cc * LANES, (cc + 1) * LANES)
                p_ref[rows, cs] = jnp.where(mask[:, cs], jnp.exp(e[:, cs] + car), 0.0).astype(BF16)
            carry_ref[rows] = car + jnp.sum(lk, axis=1, keepdims=True)
        acc_ref[...] += _dot(p_ref[...], v_ref[pl.ds(k0, tq), :])
        return c

    lax.fori_loop(0, i + 1, tile, 0)
    o_ref[...] = _unstack_heads(acc_ref[...], H_SB)


def _sb_prompt(q, kvb, upper, batch, seq):
    nq = seq // TQ_ATT
    rows = H_SB * TQ_ATT
    qmap = lambda b, i: (b * nq + i, 0)
    return pl.pallas_call(
        _sb_prompt_kernel,
        grid=(batch, nq),
        in_specs=[pl.BlockSpec((TQ_ATT, 256), qmap),
                  pl.BlockSpec((seq, 256), lambda b, i: (b, 0)),
                  pl.BlockSpec((seq, 256), lambda b, i: (b, 1)),
                  pl.BlockSpec(upper.shape, lambda b, i: (0, 0))],
        out_specs=pl.BlockSpec((TQ_ATT, 256), qmap),
        out_shape=jax.ShapeDtypeStruct((batch * seq, 256), F32),
        scratch_shapes=[pltpu.VMEM((rows, 256), BF16), pltpu.VMEM((rows, LANES), F32),
                        pltpu.VMEM((rows, 256), F32), pltpu.VMEM((rows, TQ_ATT), BF16)],
        compiler_params=_cparams(("parallel", "arbitrary")),
        name="sb_prompt",
    )(q, kvb, kvb, upper)


def _fox_prompt_kernel(q_ref, k_ref, v_ref, cq_ref, ck_ref, o_ref, q4_ref, m_ref, l_ref, acc_ref, p_ref, cqb_ref):
    i = pl.program_id(1)
    tq = TQ_ATT
    _stack_heads(q_ref[...], q4_ref, H_FOX)
    _softmax_reset(m_ref, l_ref, acc_ref)
    qpos = i * tq + lax.broadcasted_iota(jnp.int32, (tq, 1), 0)
    cq = cq_ref[...]
    for h in range(H_FOX):
        cqb_ref[h * tq:(h + 1) * tq] = jnp.broadcast_to(cq[:, GATE_COLS + h:GATE_COLS + h + 1], (tq, LANES))

    def tile(kt, c):
        k0 = pl.multiple_of(kt * tq, tq)
        s_all = _dot_nt(q4_ref[...], k_ref[pl.ds(k0, tq), :])
        mask = (k0 + lax.broadcasted_iota(jnp.int32, (tq, tq), 1)) <= qpos
        ck = ck_ref[0, kt]
        for h in range(H_FOX):
            rows = slice(h * tq, (h + 1) * tq)
            cqh = cqb_ref[rows]
            decay = jnp.concatenate([cqh - ck[h:h + 1, cc * LANES:(cc + 1) * LANES] for cc in range(tq // LANES)],
                                    axis=1)
            _flash_head(jnp.where(mask, s_all[rows] + decay, NEG), h * tq, m_ref, l_ref, acc_ref, p_ref)
        acc_ref[...] += _dot(p_ref[...], v_ref[pl.ds(k0, tq), :])
        return c

    lax.fori_loop(0, i + 1, tile, 0)
    o_ref[...] = _unstack_heads(_softmax_finish(m_ref, l_ref, acc_ref), H_FOX)


def _fox_prompt(q, kvb, c_rows, c_lanes, batch, seq):
    nq = seq // TQ_ATT
    rows = H_FOX * TQ_ATT
    qmap = lambda b, i: (b * nq + i, 0)
    return pl.pallas_call(
        _fox_prompt_kernel,
        grid=(batch, nq),
        in_specs=[pl.BlockSpec((TQ_ATT, 256), qmap),
                  pl.BlockSpec((seq, 256), lambda b, i: (b, 0)),
                  pl.BlockSpec((seq, 256), lambda b, i: (b, 1)),
                  pl.BlockSpec((TQ_ATT, LANES), qmap),
                  pl.BlockSpec((1, nq, 8, TQ_ATT), lambda b, i: (b, 0, 0, 0))],
        out_specs=pl.BlockSpec((TQ_ATT, 256), qmap),
        out_shape=jax.ShapeDtypeStruct((batch * seq, 256), F32),
        scratch_shapes=[pltpu.VMEM((rows, 256), BF16), pltpu.VMEM((rows, LANES), F32), pltpu.VMEM((rows, LANES), F32),
                        pltpu.VMEM((rows, 256), F32), pltpu.VMEM((rows, TQ_ATT), BF16),
                        pltpu.VMEM((rows, LANES), F32)],
        compiler_params=_cparams(("parallel", "arbitrary")),
        name="fox_prompt",
    )(q, kvb, kvb, c_rows, c_lanes)


CUMSUM_CHUNK = 512


def _cumsum_kernel(x_ref, tri_ref, o_ref):
    chunk = tri_ref.shape[0]
    carry = jnp.zeros((1, LANES), F32)
    for c in range(x_ref.shape[0] // chunk):
        rows = slice(c * chunk, (c + 1) * chunk)
        y = _dot_f32_left(tri_ref[...], x_ref[rows, :]) + carry
        o_ref[rows, :] = y
        carry = y[chunk - 1:chunk, :]


def _cumsum_rows(x, tri, batch, seq):
    return pl.pallas_call(
        _cumsum_kernel,
        grid=(batch,),
        in_specs=[pl.BlockSpec((seq, LANES), lambda b: (b, 0)), pl.BlockSpec(tri.shape, lambda b: (0, 0))],
        out_specs=pl.BlockSpec((seq, LANES), lambda b: (b, 0)),
        out_shape=jax.ShapeDtypeStruct(x.shape, F32),
        compiler_params=_cparams(("parallel",)),
        name="cumsum_rows",
    )(x, tri)


def _out_proj_kernel(on_ref, osb_ref, ofox_ref, g_ref, w_ref, x_ref, seg_ref, y_ref):
    seg = seg_ref[...]
    acc = x_ref[...]
    for ref, c0, width in ((on_ref, 0, 512), (osb_ref, 512, 256), (ofox_ref, 768, 256)):
        for j in range(width // 256):
            c = c0 + 256 * j
            o = _head_rms(ref[:, 256 * j:256 * (j + 1)], seg, g_ref[:, c:c + 256])
            acc = acc + _dot(o.astype(BF16), w_ref[c:c + 256, :])
    y_ref[...] = acc


def _out_proj(o_nsa, o_sb, o_fox, g, w, x, seg, tm):
    n = x.shape[0]
    row = lambda width: pl.BlockSpec((tm, width), lambda i: (i, 0))
    const = lambda a: pl.BlockSpec(a.shape, lambda i: (0, 0))
    return pl.pallas_call(
        _out_proj_kernel,
        grid=(n // tm,),
        in_specs=[row(512), row(256), row(256), const(g), const(w), row(D_MODEL), const(seg)],
        out_specs=row(D_MODEL),
        out_shape=jax.ShapeDtypeStruct(x.shape, F32),
        compiler_params=_cparams(("parallel",)),
        name="out_proj",
    )(o_nsa, o_sb, o_fox, g, w, x, seg)


def _mem_kv_kernel(x_ref, w_ref, gk_ref, seg_ref, kv_ref, kvb_ref):
    kv = _dot(x_ref[...].astype(BF16), w_ref[...])
    k = _head_rms(kv[:, :256], seg_ref[...], gk_ref[...])
    kv_ref[:, :256] = k
    kv_ref[:, 256:] = kv[:, 256:]
    kvb_ref[:, :256] = k.astype(BF16)
    kvb_ref[:, 256:] = kv[:, 256:].astype(BF16)


def _mem_kv(mem, w, gk, seg, tm):
    n = mem.shape[0]
    row = lambda width: pl.BlockSpec((tm, width), lambda i: (i, 0))
    const = lambda a: pl.BlockSpec(a.shape, lambda i: (0, 0))
    return pl.pallas_call(
        _mem_kv_kernel,
        grid=(n // tm,),
        in_specs=[row(D_MODEL), const(w), const(gk), const(seg)],
        out_specs=[row(512), row(512)],
        out_shape=[jax.ShapeDtypeStruct((n, 512), F32), jax.ShapeDtypeStruct((n, 512), BF16)],
        compiler_params=_cparams(("parallel",)),
        name="mem_kv",
    )(mem, w, gk, seg)


def _mem_q_kernel(x_ref, gn_ref, w_ref, gq_ref, seg_ref, q_ref):
    xn = _row_rms(x_ref[...], gn_ref[...]).astype(BF16)
    q = _dot(xn, w_ref[...])
    q_ref[...] = (_head_rms(q, seg_ref[...], gq_ref[...]) * SCALE).astype(q_ref.dtype)


def _mem_q(x, gn, w, gq, seg, tm, dtype):
    n = x.shape[0]
    row = lambda width: pl.BlockSpec((tm, width), lambda i: (i, 0))
    const = lambda a: pl.BlockSpec(a.shape, lambda i: (0, 0))
    return pl.pallas_call(
        _mem_q_kernel,
        grid=(n // tm,),
        in_specs=[row(D_MODEL), const(gn), const(w), const(gq), const(seg)],
        out_specs=row(256),
        out_shape=jax.ShapeDtypeStruct((n, 256), dtype),
        compiler_params=_cparams(("parallel",)),
        name="mem_q",
    )(x, gn, w, gq, seg)


def _mem_attn_kernel(q_ref, kv_ref, w_ref, x_ref, y_ref, q4_ref):
    _stack_heads(q_ref[0], q4_ref, H_MEM)
    s = _dot_nt(q4_ref[...].astype(BF16), kv_ref[0, 0, :, :256].astype(BF16))
    p = jnp.exp(s - jnp.max(s, axis=1, keepdims=True))
    p = p / jnp.sum(p, axis=1, keepdims=True)
    o = _unstack_heads(_dot(p.astype(BF16), kv_ref[0, 0, :, 256:].astype(BF16)), H_MEM)
    y_ref[0] = x_ref[0] + _dot(o.astype(BF16), w_ref[...])


def _mem_attn(q, kv, layer, w, x, tq):
    batch, seq, _ = x.shape
    return pl.pallas_call(
        _mem_attn_kernel,
        grid=(batch, seq // tq),
        in_specs=[pl.BlockSpec((1, tq, 256), lambda b, i: (b, i, 0)),
                  pl.BlockSpec((1, 1, N_MEM, 512), lambda b, i: (layer, b, 0, 0)),
                  pl.BlockSpec(w.shape, lambda b, i: (0, 0)),
                  pl.BlockSpec((1, tq, D_MODEL), lambda b, i: (b, i, 0))],
        out_specs=pl.BlockSpec((1, tq, D_MODEL), lambda b, i: (b, i, 0)),
        out_shape=jax.ShapeDtypeStruct(x.shape, F32),
        scratch_shapes=[pltpu.VMEM((H_MEM * tq, 256), q.dtype)],
        compiler_params=_cparams(("parallel", "parallel")),
        name="mem_attn",
    )(q, kv, w, x)


FF_CHUNK = 1024


def _mlp_kernel(x_ref, g_ref, wu_ref, wd_ref, y_ref, xn_ref):
    k = pl.program_id(1)

    @pl.when(k == 0)
    def _():
        x = x_ref[...]
        xn_ref[...] = _row_rms(x, g_ref[...]).astype(BF16)
        y_ref[...] = x

    h = jnp.maximum(_dot(xn_ref[...], wu_ref[...]), 0.0)
    y_ref[...] += _dot((h * h).astype(BF16), wd_ref[...])


def _mlp(x, g, w_up, w_down, tm):
    n = x.shape[0]
    return pl.pallas_call(
        _mlp_kernel,
        grid=(n // tm, D_FF // FF_CHUNK),
        in_specs=[pl.BlockSpec((tm, D_MODEL), lambda i, k: (i, 0)),
                  pl.BlockSpec(g.shape, lambda i, k: (0, 0)),
                  pl.BlockSpec((D_MODEL, FF_CHUNK), lambda i, k: (0, k)),
                  pl.BlockSpec((FF_CHUNK, D_MODEL), lambda i, k: (k, 0))],
        out_specs=pl.BlockSpec((tm, D_MODEL), lambda i, k: (i, 0)),
        out_shape=jax.ShapeDtypeStruct(x.shape, F32),
        scratch_shapes=[pltpu.VMEM((tm, D_MODEL), BF16)],
        compiler_params=_cparams(("parallel", "arbitrary")),
        name="mlp",
    )(x, g, w_up, w_down)


def _dot_f32x3(x, m):
    hi, mid, lo = _split3(x)
    return _dot(hi, m) + _dot(mid, m) + _dot(lo, m)


def _head_rows(q_row, rows=8):
    shape = (rows, q_row.shape[1])
    row = lax.broadcasted_iota(jnp.int32, shape, 0)
    return jnp.where(_lane_head(shape) == row, jnp.broadcast_to(q_row, shape), 0.0)


def _unstack_rows(acc, heads):
    head = _lane_head((1, acc.shape[1]))
    out = acc[0:1]
    for h in range(1, heads):
        out = jnp.where(head == h, acc[h:h + 1], out)
    return out


PAGES_PER_STEP = 8


def _decode_pages_kernel(pt_ref, *refs):
    del pt_ref
    n = PAGES_PER_STEP
    cmp_refs, sb_refs, fox_refs, lf_refs = refs[0:n], refs[n:2 * n], refs[2 * n:3 * n], refs[3 * n:4 * n]
    (qsb_ref, qfox_ref, foxnew_ref, lfnew_ref, wcmp_ref, u_ref, blk_ref, kc_ref, osb_ref, ofox_ref,
     qs8_ref, qf8_ref, car_ref, accs_ref, m_ref, l_ref, accf_ref, carf_ref) = refs[4 * n:]
    p = pl.program_id(1)
    last = pl.num_programs(1) - 1
    rep = lambda col: jnp.broadcast_to(col, (8, LANES))

    @pl.when(p == 0)
    def _():
        qs8_ref[...] = _head_rows(qsb_ref[0])
        qf8 = _head_rows(qfox_ref[0])
        qf8_ref[...] = qf8
        car_ref[...] = jnp.zeros(car_ref.shape, F32)
        accs_ref[...] = jnp.zeros(accs_ref.shape, F32)
        new = foxnew_ref[0]
        m_ref[...] = rep(jnp.sum(qf8 * new[:, :256], axis=1, keepdims=True))
        l_ref[...] = jnp.ones(l_ref.shape, F32)
        accf_ref[...] = jnp.broadcast_to(new[:, 256:], accf_ref.shape)
        carf_ref[...] = lfnew_ref[0]

    u = u_ref[...]
    qs8 = qs8_ref[...].astype(BF16)
    qf8 = qf8_ref[...].astype(BF16)
    e, tot, s_loc, tot_f = [], [], [], []
    for j in range(n):
        z = _dot(qs8, sb_refs[j][0, 0, :256, :].astype(BF16))
        sp = jnp.log(1.0 + jnp.exp(-jnp.abs(z)))
        lk = -(jnp.maximum(z, 0.0) + sp)
        e.append(jnp.minimum(z, 0.0) - sp + _dot_f32(lk, u))
        tot.append(jnp.sum(lk, axis=1, keepdims=True))
        lf = lf_refs[j][0, 0]
        s_loc.append(_dot(qf8, fox_refs[j][0, 0, :256, :].astype(BF16)) + _dot_f32x3(lf, u))
        tot_f.append(jnp.sum(lf, axis=1, keepdims=True))
        hi, mid, lo = _split3(cmp_refs[j][0, 0] * wcmp_ref[...])
        blk = blk_ref[...]
        c = _dot_nt(blk, hi) + _dot_nt(blk, mid) + _dot_nt(blk, lo)
        kc_ref[0, (last - p) * n + (n - 1 - j)] = c[0:kc_ref.shape[2]]

    car = car_ref[...]
    carf = carf_ref[...]
    acc_s = accs_ref[...]
    s = []
    for j in range(n):
        a = jnp.exp(e[j] + car)
        acc_s = acc_s + _dot_nt(a.astype(BF16), sb_refs[j][0, 0, 256:, :].astype(BF16))
        car = car + tot[j]
        s.append(s_loc[j] + carf)
        carf = carf + tot_f[j]
    car_ref[...] = car
    carf_ref[...] = carf
    accs_ref[...] = acc_s

    m_old = m_ref[...]
    m_new = m_old
    for j in range(n):
        m_new = jnp.maximum(m_new, jnp.max(s[j], axis=1, keepdims=True))
    alpha = jnp.exp(m_old - m_new)
    l_new = alpha * l_ref[...]
    acc_f = jnp.concatenate([alpha, alpha], axis=1) * accf_ref[...]
    for j in range(n):
        pe = jnp.exp(s[j] - m_new)
        l_new = l_new + jnp.sum(pe, axis=1, keepdims=True)
        acc_f = acc_f + _dot_nt(pe.astype(BF16), fox_refs[j][0, 0, 256:, :].astype(BF16))
    l_ref[...] = l_new
    accf_ref[...] = acc_f
    m_ref[...] = m_new

    @pl.when(p == last)
    def _():
        osb_ref[0] = _unstack_rows(accs_ref[...], H_SB)
        inv = 1.0 / l_ref[...]
        ofox_ref[0] = _unstack_rows(accf_ref[...] * jnp.concatenate([inv, inv], axis=1), H_FOX)


def _decode_pages(pt_flat, layer, cmp_pool, sb_pool, fox_pool, lf_pool, qsb, qfox, foxnew, lfnew, wcmp, upper,
                  nseq, npages):
    page = cmp_pool.shape[3]
    per_page = page // CMP_BLOCK
    n = PAGES_PER_STEP
    steps = npages // n
    blk_rows = (jnp.arange(page)[None, :] // CMP_BLOCK == jnp.arange(8)[:, None]).astype(BF16)

    def pmap(j):
        return lambda b, p, pt: (layer, pt[jnp.minimum(b, nseq - 1) * npages + (steps - 1 - p) * n + (n - 1 - j)],
                                 0, 0)

    smap = lambda b, p, pt: (b, 0, 0)
    cmap = lambda b, p, pt: (0, 0)
    paged = lambda width: [pl.BlockSpec((1, 1, width, page), pmap(j)) for j in range(n)]
    grid_spec = pltpu.PrefetchScalarGridSpec(
        num_scalar_prefetch=1,
        grid=(nseq, steps),
        in_specs=paged(256) + paged(512) + paged(512) + paged(8)
        + [pl.BlockSpec((1, 1, 256), smap),
           pl.BlockSpec((1, 1, 256), smap),
           pl.BlockSpec((1, 1, 512), smap),
           pl.BlockSpec((1, 8, LANES), smap),
           pl.BlockSpec(wcmp.shape, cmap),
           pl.BlockSpec(upper.shape, cmap),
           pl.BlockSpec(blk_rows.shape, cmap)],
        out_specs=[pl.BlockSpec((1, npages, per_page, 256), lambda b, p, pt: (b, 0, 0, 0)),
                   pl.BlockSpec((1, 1, 256), smap),
                   pl.BlockSpec((1, 1, 256), smap)],
        scratch_shapes=[pltpu.VMEM((8, 256), F32), pltpu.VMEM((8, 256), F32), pltpu.VMEM((8, LANES), F32),
                        pltpu.VMEM((8, 256), F32), pltpu.VMEM((8, LANES), F32), pltpu.VMEM((8, LANES), F32),
                        pltpu.VMEM((8, 256), F32), pltpu.VMEM((8, LANES), F32)],
    )
    pools = [cmp_pool] * n + [sb_pool] * n + [fox_pool] * n + [lf_pool] * n
    return pl.pallas_call(
        _decode_pages_kernel,
        grid_spec=grid_spec,
        out_shape=[jax.ShapeDtypeStruct((nseq, npages, per_page, 256), F32),
                   jax.ShapeDtypeStruct((nseq, 1, 256), F32),
                   jax.ShapeDtypeStruct((nseq, 1, 256), F32)],
        compiler_params=_cparams(("parallel", "arbitrary")),
        name="decode_pages",
    )(pt_flat, *pools, qsb, qfox, foxnew, lfnew, wcmp, upper, blk_rows)


def _nsa_q8(q_row):
    low = _lane_lt64((1, LANES))
    rows = []
    for g in range(KVH_NSA):
        for r in range(NSA_REP):
            t = q_row[:, r * LANES:(r + 1) * LANES]
            rows.append(jnp.where(low if g == 0 else jnp.logical_not(low), t, 0.0))
    return jnp.concatenate(rows, axis=0)


def _slope_col():
    row = lax.broadcasted_iota(jnp.int32, (H_NSA, 1), 0)
    out = jnp.zeros((H_NSA, 1), F32)
    for h in range(H_NSA):
        out = jnp.where(row == h, _nsa_slope(h), out)
    return out


def _decode_cmp_kernel(kc_ref, q_ref, gk_ref, seg_ref, pair_ref, oc_ref, idx_ref, *, past_len):
    kcr = kc_ref[0]
    nbc = kcr.shape[0]
    k = _head_rms(kcr[:, :LANES], seg_ref[...], gk_ref[...]).astype(BF16)
    q8 = _nsa_q8(q_ref[0])
    slope = _slope_col()
    blk = lax.broadcasted_iota(jnp.int32, (1, nbc), 1)
    dc = past_len - (blk * CMP_BLOCK + (CMP_BLOCK - 1))
    vis = dc >= 0
    s = jnp.where(vis, _dot_nt(q8.astype(BF16), k) - slope * dc.astype(F32), NEG)
    m = jnp.max(s, axis=1, keepdims=True)
    p = jnp.where(vis, jnp.exp(s - m), 0.0)
    p = p / jnp.maximum(jnp.sum(p, axis=1, keepdims=True), 1e-30)
    oc_ref[0] = _dot(p.astype(BF16), kcr[:, LANES:].astype(BF16))
    row = lax.broadcasted_iota(jnp.int32, (8, nbc), 0)
    imp_c = jnp.zeros((8, nbc), F32)
    for g in range(KVH_NSA):
        tot = jnp.sum(p[g * NSA_REP:(g + 1) * NSA_REP], axis=0, keepdims=True)
        imp_c = jnp.where(row == g, jnp.broadcast_to(tot, (8, nbc)), imp_c)
    imp = _dot_f32(imp_c, pair_ref[...])
    lane = lax.broadcasted_iota(jnp.int32, (8, LANES), 1)
    cur = past_len // SEL_BLOCK
    nbs = past_len // SEL_BLOCK
    forced = (lane == 0) | (lane == cur - 1)
    score = jnp.where(lane < nbs, imp + jnp.where(forced, FORCE_BONUS, 0.0), NEG)
    lanef = lane.astype(F32)
    picks = jnp.zeros((8, LANES), F32)
    for it in range(SEL_TOPK - 1):
        best = jnp.max(score, axis=1, keepdims=True)
        first = jnp.min(jnp.where(score == best, lanef, 1e9), axis=1, keepdims=True)
        picks = jnp.where(lane == it, first, picks)
        score = jnp.where(lanef == first, NEG, score)
    idx_ref[0] = picks.astype(jnp.int32)


def _decode_cmp(kc_raw, q, gk, seg1, pair, past_len):
    nseq, nbc, _ = kc_raw.shape
    smap = lambda b: (b, 0, 0)
    const = lambda a: pl.BlockSpec(a.shape, lambda b: (0, 0))
    return pl.pallas_call(
        functools.partial(_decode_cmp_kernel, past_len=past_len),
        grid=(nseq,),
        in_specs=[pl.BlockSpec((1, nbc, 256), smap), pl.BlockSpec((1, 1, 512), smap),
                  const(gk), const(seg1), const(pair)],
        out_specs=[pl.BlockSpec((1, 8, LANES), smap), pl.BlockSpec((1, 8, LANES), smap)],
        out_shape=[jax.ShapeDtypeStruct((nseq, 8, LANES), F32), jax.ShapeDtypeStruct((nseq, 8, LANES), jnp.int32)],
        compiler_params=_cparams(("parallel",)),
        name="decode_cmp",
    )(kc_raw, q, gk, seg1, pair)


N_PICK = SEL_TOPK - 1


def _decode_nsa_kernel(pt_ref, idx_ref, *refs, past_len):
    del pt_ref
    nblk = KVH_NSA * N_PICK
    blocks = refs[:nblk]
    q_ref, misc_ref, oc_ref, selnew_ref, win_ref, winnew_ref, o_ref, nwin_ref, kall_ref, vall_ref = refs[nblk:]
    b = pl.program_id(0)
    q8 = _nsa_q8(q_ref[0])
    q8b = q8.astype(BF16)
    slope = _slope_col()
    row8 = lax.broadcasted_iota(jnp.int32, (8, 1), 0)
    page = blocks[0].shape[3]
    pshift = page.bit_length() - 1
    lane = lax.broadcasted_iota(jnp.int32, (1, N_PICK * page), 1)
    slot = jnp.right_shift(lane, pshift)
    within = jnp.bitwise_and(lane, page - 1)
    new = selnew_ref[0]

    o_sel = jnp.zeros((8, LANES), F32)
    for g in range(KVH_NSA):
        blk_of_lane = jnp.zeros((1, N_PICK * page), jnp.int32)
        for j in range(N_PICK):
            pg = blocks[g * N_PICK + j][0, 0]
            kall_ref[:, j * page:(j + 1) * page] = pg[:LANES].astype(BF16)
            vall_ref[:, j * page:(j + 1) * page] = pg[LANES:].astype(BF16)
            blk_of_lane = jnp.where(slot == j, idx_ref[(b * 8 + g) * LANES + j], blk_of_lane)
        pos = jnp.left_shift(jnp.right_shift(blk_of_lane * SEL_BLOCK, pshift), pshift) + within
        ok = (jnp.right_shift(pos, 6) == blk_of_lane) & (pos <= past_len)
        s = jnp.where(ok, _dot(q8b, kall_ref[...]) - slope * (past_len - pos).astype(F32), NEG)
        s_new = jnp.sum(q8 * new[:, :LANES], axis=1, keepdims=True)
        m = jnp.maximum(jnp.max(s, axis=1, keepdims=True), s_new)
        p = jnp.where(ok, jnp.exp(s - m), 0.0)
        p_new = jnp.exp(s_new - m)
        og = (_dot_nt(p.astype(BF16), vall_ref[...]) + p_new * new[:, LANES:]) / (
            jnp.sum(p, axis=1, keepdims=True) + p_new)
        o_sel = jnp.where(jnp.right_shift(row8, 2) == g, og, o_sel)

    kw = win_ref[0, 0]
    nbuf = kw.shape[0]
    wlane = lax.broadcasted_iota(jnp.int32, (1, nbuf), 1)
    dw = nbuf - wlane
    okw = (dw <= WINDOW) & (past_len - dw >= 0)
    wnew = winnew_ref[0]
    s = jnp.where(okw, _dot_nt(q8b, kw[:, :LANES].astype(BF16)) - slope * dw.astype(F32), NEG)
    s_new = jnp.sum(q8 * wnew[:, :LANES], axis=1, keepdims=True)
    m = jnp.maximum(jnp.max(s, axis=1, keepdims=True), s_new)
    p = jnp.where(okw, jnp.exp(s - m), 0.0)
    p_new = jnp.exp(s_new - m)
    o_win = (_dot(p.astype(BF16), kw[:, LANES:].astype(BF16)) + p_new * wnew[:, LANES:]) / (
        jnp.sum(p, axis=1, keepdims=True) + p_new)
    rows = lax.broadcasted_iota(jnp.int32, (nbuf, 1), 0)
    nwin_ref[0] = jnp.where(rows == nbuf - 1, jnp.broadcast_to(wnew, kw.shape), pltpu.roll(kw, nbuf - 1, 0))

    mrow = jnp.broadcast_to(misc_ref[0], (8, LANES))
    glane = lax.broadcasted_iota(jnp.int32, (8, LANES), 1)
    gate = [jnp.sum(jnp.where(glane == 3 * row8 + br, mrow, 0.0), axis=1, keepdims=True) for br in range(3)]
    out = gate[0] * oc_ref[0] + gate[1] * o_sel + gate[2] * o_win
    low = _lane_lt64((1, LANES))
    for r in range(NSA_REP):
        o_ref[0, :, r * LANES:(r + 1) * LANES] = jnp.where(low, out[r:r + 1], out[NSA_REP + r:NSA_REP + r + 1])


def _decode_nsa(pt_flat, idx_flat, layer, sel_pool, q, misc, oc, selnew, win_buf, winnew, npages, past_len):
    nseq = q.shape[0]
    nbuf = win_buf.shape[2]
    page = sel_pool.shape[3]
    assert page & (page - 1) == 0 and page % SEL_BLOCK == 0
    per_page = page // SEL_BLOCK
    smap = lambda b, pt, idx: (b, 0, 0)

    def pick(g, j):
        def index_map(b, pt, idx):
            bb = jnp.minimum(b, nseq - 1)
            blk = jnp.clip(idx[(bb * 8 + g) * LANES + j], 0, npages * per_page - 1)
            return (layer, pt[bb * npages + blk // per_page], 0, 0)
        return pl.BlockSpec((1, 1, 256, page), index_map)

    grid_spec = pltpu.PrefetchScalarGridSpec(
        num_scalar_prefetch=2,
        grid=(nseq,),
        in_specs=[pick(g, j) for g in range(KVH_NSA) for j in range(N_PICK)]
        + [pl.BlockSpec((1, 1, 512), smap), pl.BlockSpec((1, 1, LANES), smap), pl.BlockSpec((1, 8, LANES), smap),
           pl.BlockSpec((1, 1, 256), smap), pl.BlockSpec((1, 1, nbuf, 256), lambda b, pt, idx: (layer, b, 0, 0)),
           pl.BlockSpec((1, 1, 256), smap)],
        out_specs=[pl.BlockSpec((1, 1, 512), smap), pl.BlockSpec((1, nbuf, 256), smap)],
        scratch_shapes=[pltpu.VMEM((LANES, N_PICK * page), BF16), pltpu.VMEM((LANES, N_PICK * page), BF16)],
    )
    return pl.pallas_call(
        functools.partial(_decode_nsa_kernel, past_len=past_len),
        grid_spec=grid_spec,
        out_shape=[jax.ShapeDtypeStruct((nseq, 1, 512), F32), jax.ShapeDtypeStruct((nseq, nbuf, 256), F32)],
        compiler_params=_cparams(("parallel",)),
        name="decode_nsa",
    )(pt_flat, idx_flat, *([sel_pool] * (KVH_NSA * N_PICK)), q, misc, oc, selnew, win_buf, winnew)


NSA_HEAD_ORDER = (0, 4, 1, 5, 2, 6, 3, 7)


def _layer_params(l, g_mix_norm, w_in, b_nsa_gate, b_forget, g_qk, w_cmp, g_head_out, w_out, g_mem_norm,
                  w_mem_q, w_mem_kv, g_mem_qk, w_mem_o, g_mlp_norm, w_up, w_down):
    order = jnp.asarray(NSA_HEAD_ORDER)
    wi = w_in[l]
    wq = wi[:, :512].reshape(D_MODEL, H_NSA, HEAD_DIM)[:, order].reshape(D_MODEL, 512)
    pad = jnp.zeros((D_MODEL, LANES - GATE_COLS - H_FOX), F32)
    w_in_p = jnp.concatenate([wq, wi[:, 512:1280], wi[:, 1304:2840], wi[:, 1280:1304], wi[:, 2840:2844], pad],
                             axis=1).astype(BF16)
    bias = jnp.concatenate([b_nsa_gate[l], b_forget[l], jnp.zeros((LANES - GATE_COLS - H_FOX,), F32)])[None]
    gqk = g_qk[l]
    tile = lambda v, n: jnp.tile(v, n)[None]
    head_order = jnp.concatenate([order, jnp.arange(H_NSA, H_NSA + H_SB + H_FOX)])
    return dict(
        g_mix=g_mix_norm[l][None], w_in=w_in_p, bias=bias,
        gq=tile(gqk[0], 4), gsel=tile(gqk[2], 2), gwin=tile(gqk[3], 2), gfq=tile(gqk[4], 4), gfk=tile(gqk[5], 4),
        w_cmp=jnp.repeat(jnp.transpose(w_cmp[l], (1, 0, 2)).reshape(CMP_BLOCK, 4), HEAD_DIM, axis=1),
        gk_cmp=tile(gqk[1], 2),
        g_head=g_head_out[l].reshape(-1, HEAD_DIM)[head_order].reshape(1, -1),
        w_out=w_out[l].reshape(-1, HEAD_DIM, D_MODEL)[head_order].reshape(-1, D_MODEL).astype(BF16),
        g_mem=g_mem_norm[l][None], w_mem_q=w_mem_q[l].astype(BF16), w_mem_kv=w_mem_kv[l].astype(BF16),
        gq_mem=tile(g_mem_qk[l, 0], 4), gk_mem=tile(g_mem_qk[l, 1], 4), w_mem_o=w_mem_o[l].astype(BF16),
        g_mlp=g_mlp_norm[l][None], w_up=w_up[l].astype(BF16), w_down=w_down[l].astype(BF16),
    )


def _prompt_consts(seq):
    nbc = seq // CMP_BLOCK
    n = jnp.arange(nbc)
    pair = (n[:, None] // 2 == jnp.arange(LANES)[None, :]).astype(BF16)
    kt = jnp.arange(seq // TK_SEL)[:, None, None]
    j = jnp.arange(LANES)[None, :, None]
    c = jnp.arange(TK_SEL)[None, None, :]
    expand = (j == kt * (TK_SEL // SEL_BLOCK) + c // SEL_BLOCK).astype(BF16)
    return dict(seg=_seg_matrix(256), pair=pair, expand=expand, upper=_strict_upper(TQ_ATT),
                tri=_lower_incl(CUMSUM_CHUNK))


def _prompt_layer(x, mem_rows, p, c, batch, seq):
    seg = c["seg"]
    (q, cmp, sel, selb, win, winb, misc, qsb, sb, sbb, qfox, fox, foxb) = _in_proj(
        x, p["g_mix"], p["w_in"], p["bias"], p["gq"], p["gsel"], p["gwin"], p["gfq"], p["gfk"], seg, 512)
    kc, vc = _compress(cmp, p["w_cmp"], p["gk_cmp"], seg[:LANES, :LANES], 1024)
    nbc = seq // CMP_BLOCK
    o_nsa = _nsa_prompt(q, misc, kc.reshape(batch, nbc, LANES), vc.reshape(batch, nbc, LANES), selb, winb,
                        c["pair"], c["expand"], batch, seq)
    o_sb = _sb_prompt(qsb, sbb, c["upper"], batch, seq)
    c_rows = _cumsum_rows(misc, c["tri"], batch, seq)
    nq = seq // TQ_ATT
    c_lanes = jnp.transpose(c_rows[:, GATE_COLS:GATE_COLS + 8].reshape(batch, nq, TQ_ATT, 8), (0, 1, 3, 2))
    o_fox = _fox_prompt(qfox, foxb, c_rows, c_lanes, batch, seq)
    x = _out_proj(o_nsa, o_sb, o_fox, p["g_head"], p["w_out"], x, seg, 512)
    kvm, kvmb = _mem_kv(mem_rows, p["w_mem_kv"], p["gk_mem"], seg, 256)
    qm = _mem_q(x, p["g_mem"], p["w_mem_q"], p["gq_mem"], seg, 512, BF16)
    x = _mem_attn(qm.reshape(batch, seq, 256), kvmb.reshape(1, batch, N_MEM, 512), 0, p["w_mem_o"],
                  x.reshape(batch, seq, D_MODEL), 512).reshape(batch * seq, D_MODEL)
    x = _mlp(x, p["g_mlp"], p["w_up"], p["w_down"], 1024)
    kv5 = lambda a, h: a.reshape(batch, -1, 2, h, HEAD_DIM)
    w_keep = min(WINDOW, seq)
    state = (kv5(cmp, KVH_NSA), kv5(sel, KVH_NSA), kv5(win, KVH_NSA)[:, seq - w_keep:], kv5(sb, H_SB),
             kv5(fox, H_FOX), misc[:, GATE_COLS:GATE_COLS + H_FOX].reshape(batch, seq, H_FOX), kv5(kvm, H_MEM))
    return x, state


def _sample_layer(x, pools, layer, p, c, pt_flat, npages, past_len):
    cmp_pool, sel_pool, win_buf, sb_pool, fox_pool, lf_pool, mem_kv = pools
    nseq = x.shape[0]
    seg = c["seg"]
    seg1 = seg[:LANES, :LANES]
    (q, cmp, sel, _, win, _, misc, qsb, sb, _, qfox, fox, _) = _in_proj(
        x, p["g_mix"], p["w_in"], p["bias"], p["gq"], p["gsel"], p["gwin"], p["gfq"], p["gfk"], seg, nseq)
    r3 = lambda a: a.astype(F32).reshape(nseq, 1, -1)
    logf = misc[:, GATE_COLS:GATE_COLS + H_FOX]
    lfnew = jnp.broadcast_to(jnp.pad(logf, ((0, 0), (0, 8 - H_FOX)))[:, :, None], (nseq, 8, LANES))
    page = cmp_pool.shape[3]
    w_cmp_t = jnp.tile(p["w_cmp"].T, (1, page // CMP_BLOCK))
    kc4, o_sb, o_fox = _decode_pages(pt_flat, layer, cmp_pool, sb_pool, fox_pool, lf_pool, r3(qsb), r3(qfox),
                                     r3(fox), lfnew, w_cmp_t, c["upper_page"], nseq, npages)
    kc_raw = kc4.reshape(nseq, -1, 256)
    oc, idx = _decode_cmp(kc_raw, r3(q), p["gk_cmp"], seg1, c["pair_dec"], past_len)
    o_nsa, new_win = _decode_nsa(pt_flat, idx.reshape(-1), layer, sel_pool, r3(q), r3(misc), oc, r3(sel), win_buf,
                                 r3(win), npages, past_len)
    x = _out_proj(o_nsa.reshape(nseq, 512), o_sb.reshape(nseq, 256), o_fox.reshape(nseq, 256),
                  p["g_head"], p["w_out"], x, seg, nseq)
    qm = _mem_q(x, p["g_mem"], p["w_mem_q"], p["gq_mem"], seg, nseq, F32)
    rep = lambda a: jnp.broadcast_to(a[:, None, :], (nseq, 8, a.shape[-1]))
    x = _mem_attn(rep(qm), mem_kv, layer, p["w_mem_o"], rep(x), 8)[:, 0]
    x = _mlp(x, p["g_mlp"], p["w_up"], p["w_down"], nseq)
    kv5 = lambda a, h: a.reshape(nseq, -1, 2, h, HEAD_DIM)
    state = (kv5(cmp, KVH_NSA), kv5(sel, KVH_NSA), kv5(new_win, KVH_NSA), kv5(sb, H_SB), kv5(fox, H_FOX),
             logf.reshape(nseq, 1, H_FOX))
    return x, state


def _decode_consts(npages, page):
    nbc = npages * page // CMP_BLOCK
    pair = (jnp.arange(nbc)[:, None] // 2 == jnp.arange(LANES)[None, :]).astype(BF16)
    return dict(seg=_seg_matrix(256), upper_page=_strict_upper(page), pair_dec=pair)


def kernel(x_prompt, x_sample, mem_prompt, cache_nsa_cmp, cache_nsa_sel, cache_nsa_win, cache_sb, cache_fox,
           cache_fox_logf, cache_mem, page_table, g_mix_norm, w_in, b_nsa_gate, b_forget, g_qk, w_cmp,
           g_head_out, w_out, g_mem_norm, w_mem_q, w_mem_kv, g_mem_qk, w_mem_o, g_mlp_norm, w_up, w_down):
    batch, seq, _ = x_prompt.shape
    nseq, dec_seq, _ = x_sample.shape
    assert dec_seq == 1, "the decode kernels take one new token per sequence"
    depth, n_pool, page = cache_sb.shape[:3]
    npages = page_table.shape[1]
    past_len = npages * page
    assert past_len // SEL_BLOCK <= LANES and cache_nsa_win.shape[2] <= past_len and npages % PAGES_PER_STEP == 0
    pconsts = _prompt_consts(seq)
    dconsts = _decode_consts(npages, page)
    pt_flat = page_table.reshape(-1).astype(jnp.int32)
    paged = lambda a, width: jnp.swapaxes(a.reshape(depth, n_pool, page, width), 2, 3)
    cmp_pool = paged(cache_nsa_cmp, 256)
    sel_pool = paged(cache_nsa_sel, 256)
    win_buf = cache_nsa_win.reshape(depth, nseq, -1, 256)
    sb_pool = paged(cache_sb, 512)
    fox_pool = paged(cache_fox, 512)
    lf_pool = jnp.pad(jnp.transpose(cache_fox_logf, (0, 1, 3, 2)), ((0, 0), (0, 0), (0, 8 - H_FOX), (0, 0)))
    mem_kv = cache_mem.reshape(depth, nseq, N_MEM, 512)

    xp = x_prompt.reshape(batch * seq, D_MODEL)
    xs = x_sample.reshape(nseq, D_MODEL)
    mem_rows = mem_prompt.reshape(batch * N_MEM, D_MODEL)
    pools = (cmp_pool, sel_pool, win_buf, sb_pool, fox_pool, lf_pool, mem_kv)
    pstates, sstates = [], []
    for l in range(depth):
        p = _layer_params(l, g_mix_norm, w_in, b_nsa_gate, b_forget, g_qk, w_cmp, g_head_out, w_out, g_mem_norm,
                          w_mem_q, w_mem_kv, g_mem_qk, w_mem_o, g_mlp_norm, w_up, w_down)
        xp, st = _prompt_layer(xp, mem_rows, p, pconsts, batch, seq)
        pstates.append(st)
        xs, st = _sample_layer(xs, pools, l, p, dconsts, pt_flat, npages, past_len)
        sstates.append(st)
    stack = lambda states: tuple(jnp.stack([s[k] for s in states], axis=0) for k in range(len(states[0])))
    return (xp.reshape(batch, seq, D_MODEL), xs.reshape(nseq, 1, D_MODEL)) + stack(pstates) + stack(sstates)
```

```python
import functools

import jax
import jax.numpy as jnp
from jax import lax
from jax.experimental import pallas as pl
from jax.experimental.pallas import tpu as pltpu

F32 = jnp.float32
BF16 = jnp.bfloat16

D_MODEL = 1024
HEAD_DIM = 64
LANES = 128
H_NSA, KVH_NSA, NSA_REP = 8, 2, 4
H_SB, H_FOX, H_MEM = 4, 4, 4
CMP_BLOCK, SEL_BLOCK, SEL_TOPK, WINDOW = 32, 64, 16, 512
N_MEM = 256
D_FF = 4 * D_MODEL
EPS = 1e-6
FORCE_BONUS = 1e4
SCALE = HEAD_DIM ** -0.5
NEG = -1e30
EXP_UNDERFLOW = -105.0
VMEM_LIMIT = 48 * 1024 * 1024

C_Q, C_CMP, C_SEL, C_WIN, C_SB, C_FOX, C_MISC = 0, 512, 768, 1024, 1280, 2048, 2816
IN_PAD = 2944
GATE_COLS = 3 * H_NSA


def _cparams(sem):
    return pltpu.CompilerParams(dimension_semantics=sem, vmem_limit_bytes=VMEM_LIMIT)


def _dot(a, b):
    return jnp.dot(a, b, preferred_element_type=F32)


def _dot_nt(a, b):
    return lax.dot_general(a, b, (((1,), (1,)), ((), ())), preferred_element_type=F32)


def _split2(x):
    hi = x.astype(BF16)
    lo = (x - hi.astype(F32)).astype(BF16)
    return hi, lo


def _split3(x):
    hi = x.astype(BF16)
    r = x - hi.astype(F32)
    mid = r.astype(BF16)
    lo = (r - mid.astype(F32)).astype(BF16)
    return hi, mid, lo


def _dot_f32(x, m):
    hi, lo = _split2(x)
    return _dot(hi, m) + _dot(lo, m)


def _dot_f32_left(m, x):
    hi, mid, lo = _split3(x)
    return _dot(m, hi) + _dot(m, mid) + _dot(m, lo)


def _head_rms(y, seg, g):
    ss = _dot_f32(y * y, seg)
    return y * lax.rsqrt(ss * (1.0 / HEAD_DIM) + EPS) * g


def _row_rms(x, g):
    return x * lax.rsqrt(jnp.mean(x * x, axis=-1, keepdims=True) + EPS) * g


def _log_sigmoid(z):
    return jnp.minimum(z, 0.0) - jnp.log(1.0 + jnp.exp(-jnp.abs(z)))


def _sigmoid(z):
    return 1.0 / (1.0 + jnp.exp(-z))


def _lane_lt64(shape):
    return lax.broadcasted_iota(jnp.int32, shape, len(shape) - 1) < HEAD_DIM


def _seg_matrix(n):
    i = jnp.arange(n) // HEAD_DIM
    return (i[:, None] == i[None, :]).astype(BF16)


def _strict_upper(n):
    i = jnp.arange(n)
    return (i[:, None] > i[None, :]).astype(BF16)


def _lower_incl(n):
    i = jnp.arange(n)
    return (i[None, :] <= i[:, None]).astype(BF16)


def _in_proj_kernel(x_ref, gn_ref, w_ref, bias_ref, gq_ref, gsel_ref, gwin_ref, gfq_ref, gfk_ref, seg_ref,
                    q_ref, cmp_ref, sel_ref, selb_ref, win_ref, winb_ref, misc_ref,
                    qsb_ref, sb_ref, sbb_ref, qfox_ref, fox_ref, foxb_ref):
    xn = _row_rms(x_ref[...], gn_ref[...]).astype(BF16)
    seg = seg_ref[...]
    seg1 = seg[:LANES, :LANES]

    def proj(c0, width):
        return _dot(xn, w_ref[:, c0:c0 + width])

    for j in range(2):
        q = proj(C_Q + 256 * j, 256)
        q_ref[:, 256 * j:256 * (j + 1)] = (_head_rms(q, seg, gq_ref[...]) * SCALE).astype(BF16)

    cmp_ref[...] = proj(C_CMP, 256)

    for c0, g_ref, o_ref, ob_ref in ((C_SEL, gsel_ref, sel_ref, selb_ref), (C_WIN, gwin_ref, win_ref, winb_ref)):
        kv = proj(c0, 256)
        k = _head_rms(kv[:, :LANES], seg1, g_ref[...])
        o_ref[:, :LANES] = k
        o_ref[:, LANES:] = kv[:, LANES:]
        ob_ref[:, :LANES] = k.astype(BF16)
        ob_ref[:, LANES:] = kv[:, LANES:].astype(BF16)

    z = proj(C_MISC, LANES) + bias_ref[...]
    lane = lax.broadcasted_iota(jnp.int32, z.shape, 1)
    misc_ref[...] = jnp.where(lane < GATE_COLS, _sigmoid(z), _log_sigmoid(z))

    qkv = proj(C_SB, 768)
    qsb_ref[...] = (qkv[:, :256] * SCALE).astype(BF16)
    sb_ref[...] = qkv[:, 256:]
    sbb_ref[...] = qkv[:, 256:].astype(BF16)

    qkv = proj(C_FOX, 768)
    qfox_ref[...] = (_head_rms(qkv[:, :256], seg, gfq_ref[...]) * SCALE).astype(BF16)
    k = _head_rms(qkv[:, 256:512], seg, gfk_ref[...])
    fox_ref[:, :256] = k
    fox_ref[:, 256:] = qkv[:, 512:]
    foxb_ref[:, :256] = k.astype(BF16)
    foxb_ref[:, 256:] = qkv[:, 512:].astype(BF16)


def _in_proj(x, gn, w, bias, gq, gsel, gwin, gfq, gfk, seg, tm):
    n = x.shape[0]
    row = lambda width: pl.BlockSpec((tm, width), lambda i: (i, 0))
    const = lambda a: pl.BlockSpec(a.shape, lambda i: (0, 0))
    widths = ((512, BF16), (256, F32), (256, F32), (256, BF16), (256, F32), (256, BF16), (LANES, F32),
              (256, BF16), (512, F32), (512, BF16), (256, BF16), (512, F32), (512, BF16))
    consts = (gn, w, bias, gq, gsel, gwin, gfq, gfk, seg)
    return pl.pallas_call(
        _in_proj_kernel,
        grid=(n // tm,),
        in_specs=[row(D_MODEL)] + [const(a) for a in consts],
        out_specs=[row(wd) for wd, _ in widths],
        out_shape=[jax.ShapeDtypeStruct((n, wd), dt) for wd, dt in widths],
        compiler_params=_cparams(("parallel",)),
        name="in_proj",
    )(x, *consts)


def _compress_kernel(x_ref, w_ref, gk_ref, seg_ref, kc_ref, vc_ref):
    rows = x_ref.shape[0]
    x = x_ref[...].reshape(rows // CMP_BLOCK, CMP_BLOCK, 4 * HEAD_DIM)
    c = jnp.sum(x * w_ref[...][None], axis=1)
    kc_ref[...] = _head_rms(c[:, :LANES], seg_ref[...], gk_ref[...]).astype(BF16)
    vc_ref[...] = c[:, LANES:].astype(BF16)


def _compress(kv_cmp, w_rows, gk, seg1, tr):
    n = kv_cmp.shape[0]
    nb = tr // CMP_BLOCK
    out = jax.ShapeDtypeStruct((n // CMP_BLOCK, LANES), BF16)
    return pl.pallas_call(
        _compress_kernel,
        grid=(n // tr,),
        in_specs=[pl.BlockSpec((tr, 256), lambda i: (i, 0)),
                  pl.BlockSpec(w_rows.shape, lambda i: (0, 0)),
                  pl.BlockSpec(gk.shape, lambda i: (0, 0)),
                  pl.BlockSpec(seg1.shape, lambda i: (0, 0))],
        out_specs=[pl.BlockSpec((nb, LANES), lambda i: (i, 0))] * 2,
        out_shape=[out, out],
        compiler_params=_cparams(("parallel",)),
        name="compress",
    )(kv_cmp, w_rows, gk, seg1)


TQ_NSA = 128
TK_SEL = 512
SEL_OFF = 16384.0


def _flash_head(s, r0, m_ref, l_ref, acc_ref, p_ref):
    rows, tk = s.shape
    rs = slice(r0, r0 + rows)
    m_old = m_ref[rs]
    m_new = jnp.maximum(m_old, jnp.max(s, axis=1, keepdims=True))
    psum = jnp.zeros((rows, LANES), F32)
    for c in range(tk // LANES):
        cs = slice(c * LANES, (c + 1) * LANES)
        p = jnp.exp(s[:, cs] - m_new)
        p_ref[rs, cs] = p.astype(BF16)
        psum = psum + p
    alpha = jnp.exp(m_old - m_new)
    l_ref[rs] = alpha * l_ref[rs] + jnp.sum(psum, axis=1, keepdims=True)
    for c in range(acc_ref.shape[1] // LANES):
        cs = slice(c * LANES, (c + 1) * LANES)
        acc_ref[rs, cs] = alpha * acc_ref[rs, cs]
    m_ref[rs] = m_new


def _softmax_finish(m_ref, l_ref, acc_ref):
    inv = 1.0 / jnp.maximum(l_ref[...], 1e-30)
    return jnp.concatenate([acc_ref[:, c * LANES:(c + 1) * LANES] * inv for c in range(acc_ref.shape[1] // LANES)],
                           axis=1)


def _softmax_reset(m_ref, l_ref, acc_ref):
    m_ref[...] = jnp.full(m_ref.shape, NEG, F32)
    l_ref[...] = jnp.zeros(l_ref.shape, F32)
    acc_ref[...] = jnp.zeros(acc_ref.shape, F32)


def _nsa_slope(h):
    return 2.0 ** (-(h + 1))


def _nsa_prompt_kernel(q_ref, gate_ref, kc_ref, vc_ref, ks_ref, vs_ref, kw_ref, vw_ref, pairt_ref, exp_ref,
                       o_ref, q8_ref, m_ref, l_ref, acc_ref, p_ref, oc_ref, seln_ref, used_ref, impc_ref):
    i = pl.program_id(1)
    tq = TQ_NSA
    q0 = i * tq
    nbc = kc_ref.shape[1]
    nbs = nbc // 2
    low = _lane_lt64((tq, LANES))
    for g in range(KVH_NSA):
        for r in range(NSA_REP):
            qt = q_ref[:, r * LANES:(r + 1) * LANES]
            keep = low if g == 0 else jnp.logical_not(low)
            q8_ref[(g * NSA_REP + r) * tq:(g * NSA_REP + r + 1) * tq, :] = jnp.where(keep, qt, jnp.zeros_like(qt))
    qpos = q0 + lax.broadcasted_iota(jnp.int32, (tq, 1), 0)

    s_all = _dot_nt(q8_ref[...], kc_ref[0])
    blk = lax.broadcasted_iota(jnp.int32, (tq, nbc), 1)
    dc = qpos - (blk * CMP_BLOCK + (CMP_BLOCK - 1))
    vis = dc >= 0
    dcf = dc.astype(F32)
    lane = lax.broadcasted_iota(jnp.int32, (tq, LANES), 1)
    visible = (lane * SEL_BLOCK <= qpos) & (lane < nbs)
    for g in range(KVH_NSA):
        imp_c = jnp.zeros((tq, nbc), F32)
        for r in range(NSA_REP):
            h = g * NSA_REP + r
            s = jnp.where(vis, s_all[h * tq:(h + 1) * tq] - _nsa_slope(h) * dcf, NEG)
            m = jnp.max(s, axis=1, keepdims=True)
            p = jnp.where(vis, jnp.exp(s - m), 0.0)
            p = p / jnp.maximum(jnp.sum(p, axis=1, keepdims=True), 1e-30)
            imp_c = imp_c + p
            p_ref[h * tq:(h + 1) * tq, :nbc] = p.astype(BF16)
        impc_ref[g] = imp_c
    oc_ref[...] = _dot(p_ref[:, :nbc], vc_ref[0])

    few = q0 + tq <= SEL_TOPK * SEL_BLOCK

    @pl.when(few)
    def _():
        keep = jnp.where(visible, 0.0, -SEL_OFF).astype(BF16)
        for g in range(KVH_NSA):
            seln_ref[g] = keep

    @pl.when(jnp.logical_not(few))
    def _():
        imp_t = []
        for g in range(KVH_NSA):
            hi, lo = _split2(impc_ref[g])
            imp_t.append(_dot_nt(pairt_ref[...], hi) + _dot_nt(pairt_ref[...], lo))
        imp_t = jnp.concatenate(imp_t, axis=1)
        shape = imp_t.shape
        blk_j = lax.broadcasted_iota(jnp.int32, shape, 0)
        qp = q0 + jnp.bitwise_and(lax.broadcasted_iota(jnp.int32, shape, 1), tq - 1)
        cur = jnp.right_shift(qp, 6)
        forced = (blk_j == 0) | (blk_j == cur) | (blk_j == cur - 1)
        score = jnp.where((blk_j * SEL_BLOCK <= qp) & (blk_j < nbs),
                          imp_t + jnp.where(forced, FORCE_BONUS, 0.0), NEG)
        blk_f = blk_j.astype(F32)
        sel = jnp.full(shape, -SEL_OFF, F32)
        for _ in range(min(SEL_TOPK, nbs)):
            best = jnp.max(score, axis=0, keepdims=True)
            first = jnp.min(jnp.where(score == best, blk_f, 1e9), axis=0, keepdims=True)
            hit = blk_f == first
            sel = jnp.where(hit, 0.0, sel)
            score = jnp.where(hit, NEG, score)
        for g in range(KVH_NSA):
            seln_ref[g] = sel[:, g * tq:(g + 1) * tq].T.astype(BF16)

    def branch_out():
        return _softmax_finish(m_ref, l_ref, acc_ref)

    _softmax_reset(m_ref, l_ref, acc_ref)

    used = jnp.zeros((1, LANES), F32)
    for g in range(KVH_NSA):
        used = used + (jnp.max(seln_ref[g].astype(F32), axis=0, keepdims=True) + SEL_OFF) * ((g + 1) / SEL_OFF)
    used_ref[0:1, :] = used
    blk_tile = jnp.right_shift(lax.broadcasted_iota(jnp.int32, (1, LANES), 1),
                               (TK_SEL // SEL_BLOCK).bit_length() - 1)

    def sel_tile(kt, carry):
        k0 = pl.multiple_of(kt * TK_SEL, TK_SEL)
        u = jnp.where(blk_tile == kt, used_ref[0:1, :], 0.0)
        picked = (jnp.max(jnp.where(u == 2.0, 0.0, u)).astype(jnp.int32) & 1,
                  jnp.max(jnp.where(u == 1.0, 0.0, u)).astype(jnp.int32) >> 1)
        dist = qpos - (k0 + lax.broadcasted_iota(jnp.int32, (tq, TK_SEL), 1))
        distf = dist.astype(F32)
        for g in range(KVH_NSA):
            rows = slice(g * NSA_REP * tq, (g + 1) * NSA_REP * tq)

            @pl.when(picked[g] > 0)
            def _():
                s_g = _dot_nt(q8_ref[rows, :], ks_ref[pl.ds(k0, TK_SEL), :])
                bias = jnp.where(dist >= 0, _dot(seln_ref[g], exp_ref[kt]), NEG)
                for r in range(NSA_REP):
                    h = g * NSA_REP + r
                    _flash_head(s_g[r * tq:(r + 1) * tq] - _nsa_slope(h) * distf + bias, h * tq,
                                m_ref, l_ref, acc_ref, p_ref)
                acc_ref[rows, :] += _dot(p_ref[rows, :TK_SEL], vs_ref[pl.ds(k0, TK_SEL), :])
        return carry

    lax.fori_loop(0, (q0 + tq - 1) // TK_SEL + 1, sel_tile, 0)
    o_sel = branch_out()

    wk = WINDOW + tq
    w0 = pl.multiple_of(jnp.maximum(q0 - WINDOW, 0), tq)
    s_all = _dot_nt(q8_ref[...], kw_ref[pl.ds(w0, wk), :])
    dist = qpos - (w0 + lax.broadcasted_iota(jnp.int32, (tq, wk), 1))
    distf = dist.astype(F32)
    bias = jnp.where(jnp.where(dist >= 0, dist, WINDOW + 1) <= WINDOW, 0.0, NEG)
    for h in range(H_NSA):
        rows = slice(h * tq, (h + 1) * tq)
        s = s_all[rows] - _nsa_slope(h) * distf + bias
        p = jnp.exp(s - jnp.max(s, axis=1, keepdims=True))
        l_ref[rows] = jnp.broadcast_to(jnp.sum(p, axis=1, keepdims=True), (tq, LANES))
        p_ref[rows, :wk] = p.astype(BF16)
    o_win = _dot(p_ref[:, :wk], vw_ref[pl.ds(w0, wk), :]) / l_ref[...]

    gates = gate_ref[...]
    oc = oc_ref[...]
    for r in range(NSA_REP):
        halves = []
        for g in range(KVH_NSA):
            h = g * NSA_REP + r
            rows = slice(h * tq, (h + 1) * tq)
            halves.append(gates[:, 3 * h:3 * h + 1] * oc[rows] + gates[:, 3 * h + 1:3 * h + 2] * o_sel[rows]
                          + gates[:, 3 * h + 2:3 * h + 3] * o_win[rows])
        o_ref[:, r * LANES:(r + 1) * LANES] = jnp.where(low, halves[0], halves[1])


def _nsa_prompt(q, misc, kc, vc, selb, winb, pair, expand, batch, seq):
    nq = seq // TQ_NSA
    nbc = seq // CMP_BLOCK
    rows = H_NSA * TQ_NSA
    qmap = lambda b, i: (b * nq + i, 0)
    return pl.pallas_call(
        _nsa_prompt_kernel,
        grid=(batch, nq),
        in_specs=[pl.BlockSpec((TQ_NSA, 512), qmap),
                  pl.BlockSpec((TQ_NSA, LANES), qmap),
                  pl.BlockSpec((1, nbc, LANES), lambda b, i: (b, 0, 0)),
                  pl.BlockSpec((1, nbc, LANES), lambda b, i: (b, 0, 0)),
                  pl.BlockSpec((seq, LANES), lambda b, i: (b, 0)),
                  pl.BlockSpec((seq, LANES), lambda b, i: (b, 1)),
                  pl.BlockSpec((seq, LANES), lambda b, i: (b, 0)),
                  pl.BlockSpec((seq, LANES), lambda b, i: (b, 1)),
                  pl.BlockSpec(pair.shape, lambda b, i: (0, 0)),
                  pl.BlockSpec(expand.shape, lambda b, i: (0, 0, 0))],
        out_specs=pl.BlockSpec((TQ_NSA, 512), qmap),
        out_shape=jax.ShapeDtypeStruct((batch * seq, 512), F32),
        scratch_shapes=[pltpu.VMEM((rows, LANES), BF16),
                        pltpu.VMEM((rows, LANES), F32), pltpu.VMEM((rows, LANES), F32),
                        pltpu.VMEM((rows, LANES), F32),
                        pltpu.VMEM((rows, max(TK_SEL, WINDOW + TQ_NSA)), BF16),
                        pltpu.VMEM((rows, LANES), F32),
                        pltpu.VMEM((KVH_NSA, TQ_NSA, LANES), BF16),
                        pltpu.VMEM((8, LANES), F32),
                        pltpu.VMEM((KVH_NSA, TQ_NSA, nbc), F32)],
        compiler_params=_cparams(("parallel", "arbitrary")),
        name="nsa_prompt",
    )(q, misc, kc, vc, selb, selb, winb, winb, pair, expand)


TQ_ATT = 256


def _lane_head(shape):
    return jnp.right_shift(lax.broadcasted_iota(jnp.int32, shape, len(shape) - 1), 6)


def _stack_heads(q, q4_ref, heads):
    tq = q.shape[0]
    head = _lane_head(q.shape)
    for h in range(heads):
        q4_ref[h * tq:(h + 1) * tq, :] = jnp.where(head == h, q, jnp.zeros_like(q)).astype(q4_ref.dtype)


def _unstack_heads(acc, heads):
    tq = acc.shape[0] // heads
    head = _lane_head((tq, acc.shape[1]))
    out = acc[0:tq]
    for h in range(1, heads):
        out = jnp.where(head == h, acc[h * tq:(h + 1) * tq], out)
    return out


def _sb_prompt_kernel(q_ref, k_ref, v_ref, u_ref, o_ref, q4_ref, carry_ref, acc_ref, p_ref):
    i = pl.program_id(1)
    tq = TQ_ATT
    _stack_heads(q_ref[...], q4_ref, H_SB)
    carry_ref[...] = jnp.zeros(carry_ref.shape, F32)
    acc_ref[...] = jnp.zeros(acc_ref.shape, F32)
    qpos = i * tq + lax.broadcasted_iota(jnp.int32, (tq, 1), 0)

    def tile(j):
        k0 = pl.multiple_of((i - j) * tq, tq)
        z_all = _dot_nt(q4_ref[...], k_ref[pl.ds(k0, tq), :])
        mask = (k0 + lax.broadcasted_iota(jnp.int32, (tq, tq), 1)) < qpos
        for h in range(H_SB):
            rows = slice(h * tq, (h + 1) * tq)
            z = z_all[rows]
            sp = jnp.log(1.0 + jnp.exp(-jnp.abs(z)))
            lk = jnp.where(mask, -(jnp.maximum(z, 0.0) + sp), 0.0)
            e = jnp.minimum(z, 0.0) - sp + _dot_f32(lk, u_ref[...])
            car = carry_ref[rows]
            for cc in range(tq // LANES):
                cs = slice(cc * LANES, (cc + 1) * LANES)
                p_ref[rows, cs] = jnp.where(mask[:, cs], jnp.exp(e[:, cs] + car), 0.0).astype(BF16)
            carry_ref[rows] = car + jnp.sum(lk, axis=1, keepdims=True)
        acc_ref[...] += _dot(p_ref[...], v_ref[pl.ds(k0, tq), :])
        return jnp.max(carry_ref[...])

    lax.while_loop(lambda s: (s[0] <= i) & (s[1] > EXP_UNDERFLOW), lambda s: (s[0] + 1, tile(s[0])),
                   (jnp.int32(0), jnp.float32(0.0)))
    o_ref[...] = _unstack_heads(acc_ref[...], H_SB)


def _sb_prompt(q, kvb, upper, batch, seq):
    nq = seq // TQ_ATT
    rows = H_SB * TQ_ATT
    qmap = lambda b, i: (b * nq + i, 0)
    return pl.pallas_call(
        _sb_prompt_kernel,
        grid=(batch, nq),
        in_specs=[pl.BlockSpec((TQ_ATT, 256), qmap),
                  pl.BlockSpec((seq, 256), lambda b, i: (b, 0)),
                  pl.BlockSpec((seq, 256), lambda b, i: (b, 1)),
                  pl.BlockSpec(upper.shape, lambda b, i: (0, 0))],
        out_specs=pl.BlockSpec((TQ_ATT, 256), qmap),
        out_shape=jax.ShapeDtypeStruct((batch * seq, 256), F32),
        scratch_shapes=[pltpu.VMEM((rows, 256), BF16), pltpu.VMEM((rows, LANES), F32),
                        pltpu.VMEM((rows, 256), F32), pltpu.VMEM((rows, TQ_ATT), BF16)],
        compiler_params=_cparams(("parallel", "arbitrary")),
        name="sb_prompt",
    )(q, kvb, kvb, upper)


def _fox_prompt_kernel(q_ref, k_ref, v_ref, cq_ref, ck_ref, o_ref, q4_ref, m_ref, l_ref, acc_ref, p_ref, cqb_ref):
    i = pl.program_id(1)
    tq = TQ_ATT
    _stack_heads(q_ref[...], q4_ref, H_FOX)
    _softmax_reset(m_ref, l_ref, acc_ref)
    qpos = i * tq + lax.broadcasted_iota(jnp.int32, (tq, 1), 0)
    cq = cq_ref[...]
    for h in range(H_FOX):
        cqb_ref[h * tq:(h + 1) * tq] = jnp.broadcast_to(cq[:, GATE_COLS + h:GATE_COLS + h + 1], (tq, LANES))

    def tile(kt, c):
        k0 = pl.multiple_of(kt * tq, tq)
        s_all = _dot_nt(q4_ref[...], k_ref[pl.ds(k0, tq), :])
        mask = (k0 + lax.broadcasted_iota(jnp.int32, (tq, tq), 1)) <= qpos
        ck = ck_ref[0, kt]
        for h in range(H_FOX):
            rows = slice(h * tq, (h + 1) * tq)
            cqh = cqb_ref[rows]
            decay = jnp.concatenate([cqh - ck[h:h + 1, cc * LANES:(cc + 1) * LANES] for cc in range(tq // LANES)],
                                    axis=1)
            _flash_head(jnp.where(mask, s_all[rows] + decay, NEG), h * tq, m_ref, l_ref, acc_ref, p_ref)
        acc_ref[...] += _dot(p_ref[...], v_ref[pl.ds(k0, tq), :])
        return c

    lax.fori_loop(0, i + 1, tile, 0)
    o_ref[...] = _unstack_heads(_softmax_finish(m_ref, l_ref, acc_ref), H_FOX)


def _fox_prompt(q, kvb, c_rows, c_lanes, batch, seq):
    nq = seq // TQ_ATT
    rows = H_FOX * TQ_ATT
    qmap = lambda b, i: (b * nq + i, 0)
    return pl.pallas_call(
        _fox_prompt_kernel,
        grid=(batch, nq),
        in_specs=[pl.BlockSpec((TQ_ATT, 256), qmap),
                  pl.BlockSpec((seq, 256), lambda b, i: (b, 0)),
                  pl.BlockSpec((seq, 256), lambda b, i: (b, 1)),
                  pl.BlockSpec((TQ_ATT, LANES), qmap),
                  pl.BlockSpec((1, nq, 8, TQ_ATT), lambda b, i: (b, 0, 0, 0))],
        out_specs=pl.BlockSpec((TQ_ATT, 256), qmap),
        out_shape=jax.ShapeDtypeStruct((batch * seq, 256), F32),
        scratch_shapes=[pltpu.VMEM((rows, 256), BF16), pltpu.VMEM((rows, LANES), F32), pltpu.VMEM((rows, LANES), F32),
                        pltpu.VMEM((rows, 256), F32), pltpu.VMEM((rows, TQ_ATT), BF16),
                        pltpu.VMEM((rows, LANES), F32)],
        compiler_params=_cparams(("parallel", "arbitrary")),
        name="fox_prompt",
    )(q, kvb, kvb, c_rows, c_lanes)


CUMSUM_CHUNK = 512


def _cumsum_kernel(x_ref, tri_ref, o_ref):
    chunk = tri_ref.shape[0]
    carry = jnp.zeros((1, LANES), F32)
    for c in range(x_ref.shape[0] // chunk):
        rows = slice(c * chunk, (c + 1) * chunk)
        y = _dot_f32_left(tri_ref[...], x_ref[rows, :]) + carry
        o_ref[rows, :] = y
        carry = y[chunk - 1:chunk, :]


def _cumsum_rows(x, tri, batch, seq):
    return pl.pallas_call(
        _cumsum_kernel,
        grid=(batch,),
        in_specs=[pl.BlockSpec((seq, LANES), lambda b: (b, 0)), pl.BlockSpec(tri.shape, lambda b: (0, 0))],
        out_specs=pl.BlockSpec((seq, LANES), lambda b: (b, 0)),
        out_shape=jax.ShapeDtypeStruct(x.shape, F32),
        compiler_params=_cparams(("parallel",)),
        name="cumsum_rows",
    )(x, tri)


def _out_proj_kernel(on_ref, osb_ref, ofox_ref, g_ref, w_ref, x_ref, seg_ref, y_ref):
    seg = seg_ref[...]
    acc = x_ref[...]
    for ref, c0, width in ((on_ref, 0, 512), (osb_ref, 512, 256), (ofox_ref, 768, 256)):
        for j in range(width // 256):
            c = c0 + 256 * j
            o = _head_rms(ref[:, 256 * j:256 * (j + 1)], seg, g_ref[:, c:c + 256])
            acc = acc + _dot(o.astype(BF16), w_ref[c:c + 256, :])
    y_ref[...] = acc


def _out_proj(o_nsa, o_sb, o_fox, g, w, x, seg, tm):
    n = x.shape[0]
    row = lambda width: pl.BlockSpec((tm, width), lambda i: (i, 0))
    const = lambda a: pl.BlockSpec(a.shape, lambda i: (0, 0))
    return pl.pallas_call(
        _out_proj_kernel,
        grid=(n // tm,),
        in_specs=[row(512), row(256), row(256), const(g), const(w), row(D_MODEL), const(seg)],
        out_specs=row(D_MODEL),
        out_shape=jax.ShapeDtypeStruct(x.shape, F32),
        compiler_params=_cparams(("parallel",)),
        name="out_proj",
    )(o_nsa, o_sb, o_fox, g, w, x, seg)


def _mem_kv_kernel(x_ref, w_ref, gk_ref, seg_ref, kv_ref, kvb_ref):
    kv = _dot(x_ref[...].astype(BF16), w_ref[...])
    k = _head_rms(kv[:, :256], seg_ref[...], gk_ref[...])
    kv_ref[:, :256] = k
    kv_ref[:, 256:] = kv[:, 256:]
    kvb_ref[:, :256] = k.astype(BF16)
    kvb_ref[:, 256:] = kv[:, 256:].astype(BF16)


def _mem_kv(mem, w, gk, seg, tm):
    n = mem.shape[0]
    row = lambda width: pl.BlockSpec((tm, width), lambda i: (i, 0))
    const = lambda a: pl.BlockSpec(a.shape, lambda i: (0, 0))
    return pl.pallas_call(
        _mem_kv_kernel,
        grid=(n // tm,),
        in_specs=[row(D_MODEL), const(w), const(gk), const(seg)],
        out_specs=[row(512), row(512)],
        out_shape=[jax.ShapeDtypeStruct((n, 512), F32), jax.ShapeDtypeStruct((n, 512), BF16)],
        compiler_params=_cparams(("parallel",)),
        name="mem_kv",
    )(mem, w, gk, seg)


def _mem_q_kernel(x_ref, gn_ref, w_ref, gq_ref, seg_ref, q_ref):
    xn = _row_rms(x_ref[...], gn_ref[...]).astype(BF16)
    q = _dot(xn, w_ref[...])
    q_ref[...] = (_head_rms(q, seg_ref[...], gq_ref[...]) * SCALE).astype(q_ref.dtype)


def _mem_q(x, gn, w, gq, seg, tm, dtype):
    n = x.shape[0]
    row = lambda width: pl.BlockSpec((tm, width), lambda i: (i, 0))
    const = lambda a: pl.BlockSpec(a.shape, lambda i: (0, 0))
    return pl.pallas_call(
        _mem_q_kernel,
        grid=(n // tm,),
        in_specs=[row(D_MODEL), const(gn), const(w), const(gq), const(seg)],
        out_specs=row(256),
        out_shape=jax.ShapeDtypeStruct((n, 256), dtype),
        compiler_params=_cparams(("parallel",)),
        name="mem_q",
    )(x, gn, w, gq, seg)


def _mem_attn_kernel(q_ref, kv_ref, w_ref, x_ref, y_ref, q4_ref):
    _stack_heads(q_ref[0], q4_ref, H_MEM)
    s = _dot_nt(q4_ref[...].astype(BF16), kv_ref[0, 0, :, :256].astype(BF16))
    p = jnp.exp(s - jnp.max(s, axis=1, keepdims=True))
    p = p / jnp.sum(p, axis=1, keepdims=True)
    o = _unstack_heads(_dot(p.astype(BF16), kv_ref[0, 0, :, 256:].astype(BF16)), H_MEM)
    y_ref[0] = x_ref[0] + _dot(o.astype(BF16), w_ref[...])


def _mem_attn(q, kv, layer, w, x, tq):
    batch, seq, _ = x.shape
    return pl.pallas_call(
        _mem_attn_kernel,
        grid=(batch, seq // tq),
        in_specs=[pl.BlockSpec((1, tq, 256), lambda b, i: (b, i, 0)),
                  pl.BlockSpec((1, 1, N_MEM, 512), lambda b, i: (layer, b, 0, 0)),
                  pl.BlockSpec(w.shape, lambda b, i: (0, 0)),
                  pl.BlockSpec((1, tq, D_MODEL), lambda b, i: (b, i, 0))],
        out_specs=pl.BlockSpec((1, tq, D_MODEL), lambda b, i: (b, i, 0)),
        out_shape=jax.ShapeDtypeStruct(x.shape, F32),
        scratch_shapes=[pltpu.VMEM((H_MEM * tq, 256), q.dtype)],
        compiler_params=_cparams(("parallel", "parallel")),
        name="mem_attn",
    )(q, kv, w, x)


FF_CHUNK = 1024


def _mlp_kernel(x_ref, g_ref, wu_ref, wd_ref, y_ref, xn_ref):
    k = pl.program_id(1)

    @pl.when(k == 0)
    def _():
        x = x_ref[...]
        xn_ref[...] = _row_rms(x, g_ref[...]).astype(BF16)
        y_ref[...] = x

    h = jnp.maximum(_dot(xn_ref[...], wu_ref[...]), 0.0)
    y_ref[...] += _dot((h * h).astype(BF16), wd_ref[...])


def _mlp(x, g, w_up, w_down, tm):
    n = x.shape[0]
    return pl.pallas_call(
        _mlp_kernel,
        grid=(n // tm, D_FF // FF_CHUNK),
        in_specs=[pl.BlockSpec((tm, D_MODEL), lambda i, k: (i, 0)),
                  pl.BlockSpec(g.shape, lambda i, k: (0, 0)),
                  pl.BlockSpec((D_MODEL, FF_CHUNK), lambda i, k: (0, k)),
                  pl.BlockSpec((FF_CHUNK, D_MODEL), lambda i, k: (k, 0))],
        out_specs=pl.BlockSpec((tm, D_MODEL), lambda i, k: (i, 0)),
        out_shape=jax.ShapeDtypeStruct(x.shape, F32),
        scratch_shapes=[pltpu.VMEM((tm, D_MODEL), BF16)],
        compiler_params=_cparams(("parallel", "arbitrary")),
        name="mlp",
    )(x, g, w_up, w_down)


def _dot_f32x3(x, m):
    hi, mid, lo = _split3(x)
    return _dot(hi, m) + _dot(mid, m) + _dot(lo, m)


def _head_rows(q_row, rows=8):
    shape = (rows, q_row.shape[1])
    row = lax.broadcasted_iota(jnp.int32, shape, 0)
    return jnp.where(_lane_head(shape) == row, jnp.broadcast_to(q_row, shape), 0.0)


def _unstack_rows(acc, heads):
    head = _lane_head((1, acc.shape[1]))
    out = acc[0:1]
    for h in range(1, heads):
        out = jnp.where(head == h, acc[h:h + 1], out)
    return out


PAGES_PER_STEP = 8


def _decode_pages_kernel(pt_ref, *refs):
    del pt_ref
    n = PAGES_PER_STEP
    cmp_refs, sb_refs, fox_refs, lf_refs = refs[0:n], refs[n:2 * n], refs[2 * n:3 * n], refs[3 * n:4 * n]
    (qsb_ref, qfox_ref, foxnew_ref, lfnew_ref, wcmp_ref, u_ref, blk_ref, kc_ref, osb_ref, ofox_ref,
     qs8_ref, qf8_ref, car_ref, accs_ref, m_ref, l_ref, accf_ref, carf_ref) = refs[4 * n:]
    p = pl.program_id(1)
    last = pl.num_programs(1) - 1
    rep = lambda col: jnp.broadcast_to(col, (8, LANES))

    @pl.when(p == 0)
    def _():
        qs8_ref[...] = _head_rows(qsb_ref[0])
        qf8 = _head_rows(qfox_ref[0])
        qf8_ref[...] = qf8
        car_ref[...] = jnp.zeros(car_ref.shape, F32)
        accs_ref[...] = jnp.zeros(accs_ref.shape, F32)
        new = foxnew_ref[0]
        m_ref[...] = rep(jnp.sum(qf8 * new[:, :256], axis=1, keepdims=True))
        l_ref[...] = jnp.ones(l_ref.shape, F32)
        accf_ref[...] = jnp.broadcast_to(new[:, 256:], accf_ref.shape)
        carf_ref[...] = lfnew_ref[0]

    u = u_ref[...]
    qs8 = qs8_ref[...].astype(BF16)
    qf8 = qf8_ref[...].astype(BF16)
    e, tot, s_loc, tot_f = [], [], [], []
    for j in range(n):
        z = _dot(qs8, sb_refs[j][0, 0, :256, :].astype(BF16))
        sp = jnp.log(1.0 + jnp.exp(-jnp.abs(z)))
        lk = -(jnp.maximum(z, 0.0) + sp)
        e.append(jnp.minimum(z, 0.0) - sp + _dot_f32(lk, u))
        tot.append(jnp.sum(lk, axis=1, keepdims=True))
        lf = lf_refs[j][0, 0]
        s_loc.append(_dot(qf8, fox_refs[j][0, 0, :256, :].astype(BF16)) + _dot_f32x3(lf, u))
        tot_f.append(jnp.sum(lf, axis=1, keepdims=True))
        hi, mid, lo = _split3(cmp_refs[j][0, 0] * wcmp_ref[...])
        blk = blk_ref[...]
        c = _dot_nt(blk, hi) + _dot_nt(blk, mid) + _dot_nt(blk, lo)
        kc_ref[0, (last - p) * n + (n - 1 - j)] = c[0:kc_ref.shape[2]]

    car = car_ref[...]
    carf = carf_ref[...]
    acc_s = accs_ref[...]
    s = []
    for j in range(n):
        a = jnp.exp(e[j] + car)
        acc_s = acc_s + _dot_nt(a.astype(BF16), sb_refs[j][0, 0, 256:, :].astype(BF16))
        car = car + tot[j]
        s.append(s_loc[j] + carf)
        carf = carf + tot_f[j]
    car_ref[...] = car
    carf_ref[...] = carf
    accs_ref[...] = acc_s

    m_old = m_ref[...]
    m_new = m_old
    for j in range(n):
        m_new = jnp.maximum(m_new, jnp.max(s[j], axis=1, keepdims=True))
    alpha = jnp.exp(m_old - m_new)
    l_new = alpha * l_ref[...]
    acc_f = jnp.concatenate([alpha, alpha], axis=1) * accf_ref[...]
    for j in range(n):
        pe = jnp.exp(s[j] - m_new)
        l_new = l_new + jnp.sum(pe, axis=1, keepdims=True)
        acc_f = acc_f + _dot_nt(pe.astype(BF16), fox_refs[j][0, 0, 256:, :].astype(BF16))
    l_ref[...] = l_new
    accf_ref[...] = acc_f
    m_ref[...] = m_new

    @pl.when(p == last)
    def _():
        osb_ref[0] = _unstack_rows(accs_ref[...], H_SB)
        inv = 1.0 / l_ref[...]
        ofox_ref[0] = _unstack_rows(accf_ref[...] * jnp.concatenate([inv, inv], axis=1), H_FOX)


def _decode_pages(pt_flat, layer, cmp_pool, sb_pool, fox_pool, lf_pool, qsb, qfox, foxnew, lfnew, wcmp, upper,
                  nseq, npages):
    page = cmp_pool.shape[3]
    per_page = page // CMP_BLOCK
    n = PAGES_PER_STEP
    steps = npages // n
    blk_rows = (jnp.arange(page)[None, :] // CMP_BLOCK == jnp.arange(8)[:, None]).astype(BF16)

    def pmap(j):
        return lambda b, p, pt: (layer, pt[jnp.minimum(b, nseq - 1) * npages + (steps - 1 - p) * n + (n - 1 - j)],
                                 0, 0)

    smap = lambda b, p, pt: (b, 0, 0)
    cmap = lambda b, p, pt: (0, 0)
    paged = lambda width: [pl.BlockSpec((1, 1, width, page), pmap(j)) for j in range(n)]
    grid_spec = pltpu.PrefetchScalarGridSpec(
        num_scalar_prefetch=1,
        grid=(nseq, steps),
        in_specs=paged(256) + paged(512) + paged(512) + paged(8)
        + [pl.BlockSpec((1, 1, 256), smap),
           pl.BlockSpec((1, 1, 256), smap),
           pl.BlockSpec((1, 1, 512), smap),
           pl.BlockSpec((1, 8, LANES), smap),
           pl.BlockSpec(wcmp.shape, cmap),
           pl.BlockSpec(upper.shape, cmap),
           pl.BlockSpec(blk_rows.shape, cmap)],
        out_specs=[pl.BlockSpec((1, npages, per_page, 256), lambda b, p, pt: (b, 0, 0, 0)),
                   pl.BlockSpec((1, 1, 256), smap),
                   pl.BlockSpec((1, 1, 256), smap)],
        scratch_shapes=[pltpu.VMEM((8, 256), F32), pltpu.VMEM((8, 256), F32), pltpu.VMEM((8, LANES), F32),
                        pltpu.VMEM((8, 256), F32), pltpu.VMEM((8, LANES), F32), pltpu.VMEM((8, LANES), F32),
                        pltpu.VMEM((8, 256), F32), pltpu.VMEM((8, LANES), F32)],
    )
    pools = [cmp_pool] * n + [sb_pool] * n + [fox_pool] * n + [lf_pool] * n
    return pl.pallas_call(
        _decode_pages_kernel,
        grid_spec=grid_spec,
        out_shape=[jax.ShapeDtypeStruct((nseq, npages, per_page, 256), F32),
                   jax.ShapeDtypeStruct((nseq, 1, 256), F32),
                   jax.ShapeDtypeStruct((nseq, 1, 256), F32)],
        compiler_params=_cparams(("parallel", "arbitrary")),
        name="decode_pages",
    )(pt_flat, *pools, qsb, qfox, foxnew, lfnew, wcmp, upper, blk_rows)


def _nsa_q8(q_row):
    low = _lane_lt64((1, LANES))
    rows = []
    for g in range(KVH_NSA):
        for r in range(NSA_REP):
            t = q_row[:, r * LANES:(r + 1) * LANES]
            rows.append(jnp.where(low if g == 0 else jnp.logical_not(low), t, 0.0))
    return jnp.concatenate(rows, axis=0)


def _slope_col():
    row = lax.broadcasted_iota(jnp.int32, (H_NSA, 1), 0)
    out = jnp.zeros((H_NSA, 1), F32)
    for h in range(H_NSA):
        out = jnp.where(row == h, _nsa_slope(h), out)
    return out


def _decode_cmp_kernel(kc_ref, q_ref, gk_ref, seg_ref, pair_ref, oc_ref, idx_ref, *, past_len):
    kcr = kc_ref[0]
    nbc = kcr.shape[0]
    k = _head_rms(kcr[:, :LANES], seg_ref[...], gk_ref[...]).astype(BF16)
    q8 = _nsa_q8(q_ref[0])
    slope = _slope_col()
    blk = lax.broadcasted_iota(jnp.int32, (1, nbc), 1)
    dc = past_len - (blk * CMP_BLOCK + (CMP_BLOCK - 1))
    vis = dc >= 0
    s = jnp.where(vis, _dot_nt(q8.astype(BF16), k) - slope * dc.astype(F32), NEG)
    m = jnp.max(s, axis=1, keepdims=True)
    p = jnp.where(vis, jnp.exp(s - m), 0.0)
    p = p / jnp.maximum(jnp.sum(p, axis=1, keepdims=True), 1e-30)
    oc_ref[0] = _dot(p.astype(BF16), kcr[:, LANES:].astype(BF16))
    row = lax.broadcasted_iota(jnp.int32, (8, nbc), 0)
    imp_c = jnp.zeros((8, nbc), F32)
    for g in range(KVH_NSA):
        tot = jnp.sum(p[g * NSA_REP:(g + 1) * NSA_REP], axis=0, keepdims=True)
        imp_c = jnp.where(row == g, jnp.broadcast_to(tot, (8, nbc)), imp_c)
    imp = _dot_f32(imp_c, pair_ref[...])
    lane = lax.broadcasted_iota(jnp.int32, (8, LANES), 1)
    cur = past_len // SEL_BLOCK
    nbs = past_len // SEL_BLOCK
    forced = (lane == 0) | (lane == cur - 1)
    score = jnp.where(lane < nbs, imp + jnp.where(forced, FORCE_BONUS, 0.0), NEG)
    lanef = lane.astype(F32)
    picks = jnp.zeros((8, LANES), F32)
    for it in range(SEL_TOPK - 1):
        best = jnp.max(score, axis=1, keepdims=True)
        first = jnp.min(jnp.where(score == best, lanef, 1e9), axis=1, keepdims=True)
        picks = jnp.where(lane == it, first, picks)
        score = jnp.where(lanef == first, NEG, score)
    idx_ref[0] = picks.astype(jnp.int32)


def _decode_cmp(kc_raw, q, gk, seg1, pair, past_len):
    nseq, nbc, _ = kc_raw.shape
    smap = lambda b: (b, 0, 0)
    const = lambda a: pl.BlockSpec(a.shape, lambda b: (0, 0))
    return pl.pallas_call(
        functools.partial(_decode_cmp_kernel, past_len=past_len),
        grid=(nseq,),
        in_specs=[pl.BlockSpec((1, nbc, 256), smap), pl.BlockSpec((1, 1, 512), smap),
                  const(gk), const(seg1), const(pair)],
        out_specs=[pl.BlockSpec((1, 8, LANES), smap), pl.BlockSpec((1, 8, LANES), smap)],
        out_shape=[jax.ShapeDtypeStruct((nseq, 8, LANES), F32), jax.ShapeDtypeStruct((nseq, 8, LANES), jnp.int32)],
        compiler_params=_cparams(("parallel",)),
        name="decode_cmp",
    )(kc_raw, q, gk, seg1, pair)


N_PICK = SEL_TOPK - 1


def _decode_nsa_kernel(pt_ref, idx_ref, *refs, past_len):
    del pt_ref
    nblk = KVH_NSA * N_PICK
    blocks = refs[:nblk]
    q_ref, misc_ref, oc_ref, selnew_ref, win_ref, winnew_ref, o_ref, nwin_ref, kall_ref, vall_ref = refs[nblk:]
    b = pl.program_id(0)
    q8 = _nsa_q8(q_ref[0])
    q8b = q8.astype(BF16)
    slope = _slope_col()
    row8 = lax.broadcasted_iota(jnp.int32, (8, 1), 0)
    page = blocks[0].shape[3]
    pshift = page.bit_length() - 1
    lane = lax.broadcasted_iota(jnp.int32, (1, N_PICK * page), 1)
    slot = jnp.right_shift(lane, pshift)
    within = jnp.bitwise_and(lane, page - 1)
    new = selnew_ref[0]

    o_sel = jnp.zeros((8, LANES), F32)
    for g in range(KVH_NSA):
        blk_of_lane = jnp.zeros((1, N_PICK * page), jnp.int32)
        for j in range(N_PICK):
            pg = blocks[g * N_PICK + j][0, 0]
            kall_ref[:, j * page:(j + 1) * page] = pg[:LANES].astype(BF16)
            vall_ref[:, j * page:(j + 1) * page] = pg[LANES:].astype(BF16)
            blk_of_lane = jnp.where(slot == j, idx_ref[(b * 8 + g) * LANES + j], blk_of_lane)
        pos = jnp.left_shift(jnp.right_shift(blk_of_lane * SEL_BLOCK, pshift), pshift) + within
        ok = (jnp.right_shift(pos, 6) == blk_of_lane) & (pos <= past_len)
        s = jnp.where(ok, _dot(q8b, kall_ref[...]) - slope * (past_len - pos).astype(F32), NEG)
        s_new = jnp.sum(q8 * new[:, :LANES], axis=1, keepdims=True)
        m = jnp.maximum(jnp.max(s, axis=1, keepdims=True), s_new)
        p = jnp.where(ok, jnp.exp(s - m), 0.0)
        p_new = jnp.exp(s_new - m)
        og = (_dot_nt(p.astype(BF16), vall_ref[...]) + p_new * new[:, LANES:]) / (
            jnp.sum(p, axis=1, keepdims=True) + p_new)
        o_sel = jnp.where(jnp.right_shift(row8, 2) == g, og, o_sel)

    kw = win_ref[0, 0]
    nbuf = kw.shape[0]
    wlane = lax.broadcasted_iota(jnp.int32, (1, nbuf), 1)
    dw = nbuf - wlane
    okw = (dw <= WINDOW) & (past_len - dw >= 0)
    wnew = winnew_ref[0]
    s = jnp.where(okw, _dot_nt(q8b, kw[:, :LANES].astype(BF16)) - slope * dw.astype(F32), NEG)
    s_new = jnp.sum(q8 * wnew[:, :LANES], axis=1, keepdims=True)
    m = jnp.maximum(jnp.max(s, axis=1, keepdims=True), s_new)
    p = jnp.where(okw, jnp.exp(s - m), 0.0)
    p_new = jnp.exp(s_new - m)
    o_win = (_dot(p.astype(BF16), kw[:, LANES:].astype(BF16)) + p_new * wnew[:, LANES:]) / (
        jnp.sum(p, axis=1, keepdims=True) + p_new)
    rows = lax.broadcasted_iota(jnp.int32, (nbuf, 1), 0)
    nwin_ref[0] = jnp.where(rows == nbuf - 1, jnp.broadcast_to(wnew, kw.shape), pltpu.roll(kw, nbuf - 1, 0))

    mrow = jnp.broadcast_to(misc_ref[0], (8, LANES))
    glane = lax.broadcasted_iota(jnp.int32, (8, LANES), 1)
    gate = [jnp.sum(jnp.where(glane == 3 * row8 + br, mrow, 0.0), axis=1, keepdims=True) for br in range(3)]
    out = gate[0] * oc_ref[0] + gate[1] * o_sel + gate[2] * o_win
    low = _lane_lt64((1, LANES))
    for r in range(NSA_REP):
        o_ref[0, :, r * LANES:(r + 1) * LANES] = jnp.where(low, out[r:r + 1], out[NSA_REP + r:NSA_REP + r + 1])


def _decode_nsa(pt_flat, idx_flat, layer, sel_pool, q, misc, oc, selnew, win_buf, winnew, npages, past_len):
    nseq = q.shape[0]
    nbuf = win_buf.shape[2]
    page = sel_pool.shape[3]
    assert page & (page - 1) == 0 and page % SEL_BLOCK == 0
    per_page = page // SEL_BLOCK
    smap = lambda b, pt, idx: (b, 0, 0)

    def pick(g, j):
        def index_map(b, pt, idx):
            bb = jnp.minimum(b, nseq - 1)
            blk = jnp.clip(idx[(bb * 8 + g) * LANES + j], 0, npages * per_page - 1)
            return (layer, pt[bb * npages + blk // per_page], 0, 0)
        return pl.BlockSpec((1, 1, 256, page), index_map)

    grid_spec = pltpu.PrefetchScalarGridSpec(
        num_scalar_prefetch=2,
        grid=(nseq,),
        in_specs=[pick(g, j) for g in range(KVH_NSA) for j in range(N_PICK)]
        + [pl.BlockSpec((1, 1, 512), smap), pl.BlockSpec((1, 1, LANES), smap), pl.BlockSpec((1, 8, LANES), smap),
           pl.BlockSpec((1, 1, 256), smap), pl.BlockSpec((1, 1, nbuf, 256), lambda b, pt, idx: (layer, b, 0, 0)),
           pl.BlockSpec((1, 1, 256), smap)],
        out_specs=[pl.BlockSpec((1, 1, 512), smap), pl.BlockSpec((1, nbuf, 256), smap)],
        scratch_shapes=[pltpu.VMEM((LANES, N_PICK * page), BF16), pltpu.VMEM((LANES, N_PICK * page), BF16)],
    )
    return pl.pallas_call(
        functools.partial(_decode_nsa_kernel, past_len=past_len),
        grid_spec=grid_spec,
        out_shape=[jax.ShapeDtypeStruct((nseq, 1, 512), F32), jax.ShapeDtypeStruct((nseq, nbuf, 256), F32)],
        compiler_params=_cparams(("parallel",)),
        name="decode_nsa",
    )(pt_flat, idx_flat, *([sel_pool] * (KVH_NSA * N_PICK)), q, misc, oc, selnew, win_buf, winnew)


NSA_HEAD_ORDER = (0, 4, 1, 5, 2, 6, 3, 7)


def _layer_params(l, g_mix_norm, w_in, b_nsa_gate, b_forget, g_qk, w_cmp, g_head_out, w_out, g_mem_norm,
                  w_mem_q, w_mem_kv, g_mem_qk, w_mem_o, g_mlp_norm, w_up, w_down):
    order = jnp.asarray(NSA_HEAD_ORDER)
    wi = w_in[l]
    wq = wi[:, :512].reshape(D_MODEL, H_NSA, HEAD_DIM)[:, order].reshape(D_MODEL, 512)
    pad = jnp.zeros((D_MODEL, LANES - GATE_COLS - H_FOX), F32)
    w_in_p = jnp.concatenate([wq, wi[:, 512:1280], wi[:, 1304:2840], wi[:, 1280:1304], wi[:, 2840:2844], pad],
                             axis=1).astype(BF16)
    bias = jnp.concatenate([b_nsa_gate[l], b_forget[l], jnp.zeros((LANES - GATE_COLS - H_FOX,), F32)])[None]
    gqk = g_qk[l]
    tile = lambda v, n: jnp.tile(v, n)[None]
    head_order = jnp.concatenate([order, jnp.arange(H_NSA, H_NSA + H_SB + H_FOX)])
    return dict(
        g_mix=g_mix_norm[l][None], w_in=w_in_p, bias=bias,
        gq=tile(gqk[0], 4), gsel=tile(gqk[2], 2), gwin=tile(gqk[3], 2), gfq=tile(gqk[4], 4), gfk=tile(gqk[5], 4),
        w_cmp=jnp.repeat(jnp.transpose(w_cmp[l], (1, 0, 2)).reshape(CMP_BLOCK, 4), HEAD_DIM, axis=1),
        gk_cmp=tile(gqk[1], 2),
        g_head=g_head_out[l].reshape(-1, HEAD_DIM)[head_order].reshape(1, -1),
        w_out=w_out[l].reshape(-1, HEAD_DIM, D_MODEL)[head_order].reshape(-1, D_MODEL).astype(BF16),
        g_mem=g_mem_norm[l][None], w_mem_q=w_mem_q[l].astype(BF16), w_mem_kv=w_mem_kv[l].astype(BF16),
        gq_mem=tile(g_mem_qk[l, 0], 4), gk_mem=tile(g_mem_qk[l, 1], 4), w_mem_o=w_mem_o[l].astype(BF16),
        g_mlp=g_mlp_norm[l][None], w_up=w_up[l].astype(BF16), w_down=w_down[l].astype(BF16),
    )


def _prompt_consts(seq):
    nbc = seq // CMP_BLOCK
    n = jnp.arange(nbc)
    pair = (jnp.arange(LANES)[:, None] == n[None, :] // 2).astype(BF16)
    kt = jnp.arange(seq // TK_SEL)[:, None, None]
    j = jnp.arange(LANES)[None, :, None]
    c = jnp.arange(TK_SEL)[None, None, :]
    expand = (j == kt * (TK_SEL // SEL_BLOCK) + c // SEL_BLOCK).astype(BF16)
    return dict(seg=_seg_matrix(256), pair=pair, expand=expand, upper=_strict_upper(TQ_ATT),
                tri=_lower_incl(CUMSUM_CHUNK))


def _prompt_layer(x, mem_rows, p, c, batch, seq):
    seg = c["seg"]
    (q, cmp, sel, selb, win, winb, misc, qsb, sb, sbb, qfox, fox, foxb) = _in_proj(
        x, p["g_mix"], p["w_in"], p["bias"], p["gq"], p["gsel"], p["gwin"], p["gfq"], p["gfk"], seg, 512)
    kc, vc = _compress(cmp, p["w_cmp"], p["gk_cmp"], seg[:LANES, :LANES], 1024)
    nbc = seq // CMP_BLOCK
    o_nsa = _nsa_prompt(q, misc, kc.reshape(batch, nbc, LANES), vc.reshape(batch, nbc, LANES), selb, winb,
                        c["pair"], c["expand"], batch, seq)
    o_sb = _sb_prompt(qsb, sbb, c["upper"], batch, seq)
    c_rows = _cumsum_rows(misc, c["tri"], batch, seq)
    nq = seq // TQ_ATT
    c_lanes = jnp.transpose(c_rows[:, GATE_COLS:GATE_COLS + 8].reshape(batch, nq, TQ_ATT, 8), (0, 1, 3, 2))
    o_fox = _fox_prompt(qfox, foxb, c_rows, c_lanes, batch, seq)
    x = _out_proj(o_nsa, o_sb, o_fox, p["g_head"], p["w_out"], x, seg, 512)
    kvm, kvmb = _mem_kv(mem_rows, p["w_mem_kv"], p["gk_mem"], seg, 256)
    qm = _mem_q(x, p["g_mem"], p["w_mem_q"], p["gq_mem"], seg, 512, BF16)
    x = _mem_attn(qm.reshape(batch, seq, 256), kvmb.reshape(1, batch, N_MEM, 512), 0, p["w_mem_o"],
                  x.reshape(batch, seq, D_MODEL), 512).reshape(batch * seq, D_MODEL)
    x = _mlp(x, p["g_mlp"], p["w_up"], p["w_down"], 1024)
    kv5 = lambda a, h: a.reshape(batch, -1, 2, h, HEAD_DIM)
    w_keep = min(WINDOW, seq)
    state = (kv5(cmp, KVH_NSA), kv5(sel, KVH_NSA), kv5(win, KVH_NSA)[:, seq - w_keep:], kv5(sb, H_SB),
             kv5(fox, H_FOX), misc[:, GATE_COLS:GATE_COLS + H_FOX].reshape(batch, seq, H_FOX), kv5(kvm, H_MEM))
    return x, state


def _sample_layer(x, pools, layer, p, c, pt_flat, npages, past_len):
    cmp_pool, sel_pool, win_buf, sb_pool, fox_pool, lf_pool, mem_kv = pools
    nseq = x.shape[0]
    seg = c["seg"]
    seg1 = seg[:LANES, :LANES]
    (q, cmp, sel, _, win, _, misc, qsb, sb, _, qfox, fox, _) = _in_proj(
        x, p["g_mix"], p["w_in"], p["bias"], p["gq"], p["gsel"], p["gwin"], p["gfq"], p["gfk"], seg, nseq)
    r3 = lambda a: a.astype(F32).reshape(nseq, 1, -1)
    logf = misc[:, GATE_COLS:GATE_COLS + H_FOX]
    lfnew = jnp.broadcast_to(jnp.pad(logf, ((0, 0), (0, 8 - H_FOX)))[:, :, None], (nseq, 8, LANES))
    page = cmp_pool.shape[3]
    w_cmp_t = jnp.tile(p["w_cmp"].T, (1, page // CMP_BLOCK))
    kc4, o_sb, o_fox = _decode_pages(pt_flat, layer, cmp_pool, sb_pool, fox_pool, lf_pool, r3(qsb), r3(qfox),
                                     r3(fox), lfnew, w_cmp_t, c["upper_page"], nseq, npages)
    kc_raw = kc4.reshape(nseq, -1, 256)
    oc, idx = _decode_cmp(kc_raw, r3(q), p["gk_cmp"], seg1, c["pair_dec"], past_len)
    o_nsa, new_win = _decode_nsa(pt_flat, idx.reshape(-1), layer, sel_pool, r3(q), r3(misc), oc, r3(sel), win_buf,
                                 r3(win), npages, past_len)
    x = _out_proj(o_nsa.reshape(nseq, 512), o_sb.reshape(nseq, 256), o_fox.reshape(nseq, 256),
                  p["g_head"], p["w_out"], x, seg, nseq)
    qm = _mem_q(x, p["g_mem"], p["w_mem_q"], p["gq_mem"], seg, nseq, F32)
    rep = lambda a: jnp.broadcast_to(a[:, None, :], (nseq, 8, a.shape[-1]))
    x = _mem_attn(rep(qm), mem_kv, layer, p["w_mem_o"], rep(x), 8)[:, 0]
    x = _mlp(x, p["g_mlp"], p["w_up"], p["w_down"], nseq)
    kv5 = lambda a, h: a.reshape(nseq, -1, 2, h, HEAD_DIM)
    state = (kv5(cmp, KVH_NSA), kv5(sel, KVH_NSA), kv5(new_win, KVH_NSA), kv5(sb, H_SB), kv5(fox, H_FOX),
             logf.reshape(nseq, 1, H_FOX))
    return x, state


def _decode_consts(npages, page):
    nbc = npages * page // CMP_BLOCK
    pair = (jnp.arange(nbc)[:, None] // 2 == jnp.arange(LANES)[None, :]).astype(BF16)
    return dict(seg=_seg_matrix(256), upper_page=_strict_upper(page), pair_dec=pair)


def kernel(x_prompt, x_sample, mem_prompt, cache_nsa_cmp, cache_nsa_sel, cache_nsa_win, cache_sb, cache_fox,
           cache_fox_logf, cache_mem, page_table, g_mix_norm, w_in, b_nsa_gate, b_forget, g_qk, w_cmp,
           g_head_out, w_out, g_mem_norm, w_mem_q, w_mem_kv, g_mem_qk, w_mem_o, g_mlp_norm, w_up, w_down):
    batch, seq, _ = x_prompt.shape
    nseq, dec_seq, _ = x_sample.shape
    assert dec_seq == 1, "the decode kernels take one new token per sequence"
    depth, n_pool, page = cache_sb.shape[:3]
    npages = page_table.shape[1]
    past_len = npages * page
    assert past_len // SEL_BLOCK <= LANES and cache_nsa_win.shape[2] <= past_len and npages % PAGES_PER_STEP == 0
    pconsts = _prompt_consts(seq)
    dconsts = _decode_consts(npages, page)
    pt_flat = page_table.reshape(-1).astype(jnp.int32)
    paged = lambda a, width: jnp.swapaxes(a.reshape(depth, n_pool, page, width), 2, 3)
    cmp_pool = paged(cache_nsa_cmp, 256)
    sel_pool = paged(cache_nsa_sel, 256)
    win_buf = cache_nsa_win.reshape(depth, nseq, -1, 256)
    sb_pool = paged(cache_sb, 512)
    fox_pool = paged(cache_fox, 512)
    lf_pool = jnp.pad(jnp.transpose(cache_fox_logf, (0, 1, 3, 2)), ((0, 0), (0, 0), (0, 8 - H_FOX), (0, 0)))
    mem_kv = cache_mem.reshape(depth, nseq, N_MEM, 512)

    xp = x_prompt.reshape(batch * seq, D_MODEL)
    xs = x_sample.reshape(nseq, D_MODEL)
    mem_rows = mem_prompt.reshape(batch * N_MEM, D_MODEL)
    pools = (cmp_pool, sel_pool, win_buf, sb_pool, fox_pool, lf_pool, mem_kv)
    pstates, sstates = [], []
    for l in range(depth):
        p = _layer_params(l, g_mix_norm, w_in, b_nsa_gate, b_forget, g_qk, w_cmp, g_head_out, w_out, g_mem_norm,
                          w_mem_q, w_mem_kv, g_mem_qk, w_mem_o, g_mlp_norm, w_up, w_down)
        xp, st = _prompt_layer(xp, mem_rows, p, pconsts, batch, seq)
        pstates.append(st)
        xs, st = _sample_layer(xs, pools, l, p, dconsts, pt_flat, npages, past_len)
        sstates.append(st)
    stack = lambda states: tuple(jnp.stack([s[k] for s in states], axis=0) for k in range(len(states[0])))
    return (xp.reshape(batch, seq, D_MODEL), xs.reshape(nseq, 1, D_MODEL)) + stack(pstates) + stack(sstates)
```

```python
import functools

import jax
import jax.numpy as jnp
from jax import lax
from jax.experimental import pallas as pl
from jax.experimental.pallas import tpu as pltpu

F32 = jnp.float32
BF16 = jnp.bfloat16

D_MODEL = 1024
HEAD_DIM = 64
LANES = 128
H_NSA, KVH_NSA, NSA_REP = 8, 2, 4
H_SB, H_FOX, H_MEM = 4, 4, 4
CMP_BLOCK, SEL_BLOCK, SEL_TOPK, WINDOW = 32, 64, 16, 512
N_MEM = 256
D_FF = 4 * D_MODEL
EPS = 1e-6
FORCE_BONUS = 1e4
SCALE = HEAD_DIM ** -0.5
NEG = -1e30
EXP_UNDERFLOW = -105.0
VMEM_LIMIT = 48 * 1024 * 1024

C_Q, C_CMP, C_SEL, C_WIN, C_SB, C_FOX, C_MISC = 0, 512, 768, 1024, 1280, 2048, 2816
IN_PAD = 2944
GATE_COLS = 3 * H_NSA


def _cparams(sem):
    return pltpu.CompilerParams(dimension_semantics=sem, vmem_limit_bytes=VMEM_LIMIT)


def _dot(a, b):
    return jnp.dot(a, b, preferred_element_type=F32)


def _dot_nt(a, b):
    return lax.dot_general(a, b, (((1,), (1,)), ((), ())), preferred_element_type=F32)


def _split2(x):
    hi = x.astype(BF16)
    lo = (x - hi.astype(F32)).astype(BF16)
    return hi, lo


def _split3(x):
    hi = x.astype(BF16)
    r = x - hi.astype(F32)
    mid = r.astype(BF16)
    lo = (r - mid.astype(F32)).astype(BF16)
    return hi, mid, lo


def _dot_f32(x, m):
    hi, lo = _split2(x)
    return _dot(hi, m) + _dot(lo, m)


def _dot_f32_left(m, x):
    hi, mid, lo = _split3(x)
    return _dot(m, hi) + _dot(m, mid) + _dot(m, lo)


def _head_rms(y, seg, g):
    ss = _dot_f32(y * y, seg)
    return y * lax.rsqrt(ss * (1.0 / HEAD_DIM) + EPS) * g


def _row_rms(x, g):
    return x * lax.rsqrt(jnp.mean(x * x, axis=-1, keepdims=True) + EPS) * g


def _log_sigmoid(z):
    return jnp.minimum(z, 0.0) - jnp.log(1.0 + jnp.exp(-jnp.abs(z)))


def _sigmoid(z):
    return 1.0 / (1.0 + jnp.exp(-z))


def _lane_lt64(shape):
    return lax.broadcasted_iota(jnp.int32, shape, len(shape) - 1) < HEAD_DIM


def _seg_matrix(n):
    i = jnp.arange(n) // HEAD_DIM
    return (i[:, None] == i[None, :]).astype(BF16)


def _strict_upper(n):
    i = jnp.arange(n)
    return (i[:, None] > i[None, :]).astype(BF16)


def _lower_incl(n):
    i = jnp.arange(n)
    return (i[None, :] <= i[:, None]).astype(BF16)


def _in_proj_kernel(x_ref, gn_ref, w_ref, bias_ref, gq_ref, gsel_ref, gwin_ref, gfq_ref, gfk_ref, seg_ref,
                    q_ref, cmp_ref, sel_ref, selb_ref, win_ref, winb_ref, misc_ref,
                    qsb_ref, sb_ref, sbb_ref, qfox_ref, fox_ref, foxb_ref):
    xn = _row_rms(x_ref[...], gn_ref[...]).astype(BF16)
    seg = seg_ref[...]
    seg1 = seg[:LANES, :LANES]

    def proj(c0, width):
        return _dot(xn, w_ref[:, c0:c0 + width])

    def state(ref, c0, val):
        if len(ref.shape) == 2:
            ref[:, c0:c0 + val.shape[1]] = val
        else:
            ref[0, c0:c0 + val.shape[1], :] = val.T

    for j in range(2):
        q = proj(C_Q + 256 * j, 256)
        q_ref[:, 256 * j:256 * (j + 1)] = (_head_rms(q, seg, gq_ref[...]) * SCALE).astype(BF16)

    state(cmp_ref, 0, proj(C_CMP, 256))

    for c0, g_ref, o_ref, ob_ref in ((C_SEL, gsel_ref, sel_ref, selb_ref), (C_WIN, gwin_ref, win_ref, winb_ref)):
        kv = proj(c0, 256)
        k = _head_rms(kv[:, :LANES], seg1, g_ref[...])
        state(o_ref, 0, k)
        state(o_ref, LANES, kv[:, LANES:])
        ob_ref[:, :LANES] = k.astype(BF16)
        ob_ref[:, LANES:] = kv[:, LANES:].astype(BF16)

    z = proj(C_MISC, LANES) + bias_ref[...]
    lane = lax.broadcasted_iota(jnp.int32, z.shape, 1)
    misc_ref[...] = jnp.where(lane < GATE_COLS, _sigmoid(z), _log_sigmoid(z))

    qkv = proj(C_SB, 768)
    qsb_ref[...] = (qkv[:, :256] * SCALE).astype(BF16)
    state(sb_ref, 0, qkv[:, 256:512])
    state(sb_ref, 256, qkv[:, 512:])
    sbb_ref[...] = qkv[:, 256:].astype(BF16)

    qkv = proj(C_FOX, 768)
    qfox_ref[...] = (_head_rms(qkv[:, :256], seg, gfq_ref[...]) * SCALE).astype(BF16)
    k = _head_rms(qkv[:, 256:512], seg, gfk_ref[...])
    state(fox_ref, 0, k)
    state(fox_ref, 256, qkv[:, 512:])
    foxb_ref[:, :256] = k.astype(BF16)
    foxb_ref[:, 256:] = qkv[:, 512:].astype(BF16)


IN_PROJ_OUT = ((512, BF16, False), (256, F32, True), (256, F32, True), (256, BF16, False), (256, F32, True),
               (256, BF16, False), (LANES, F32, False), (256, BF16, False), (512, F32, True), (512, BF16, False),
               (256, BF16, False), (512, F32, True), (512, BF16, False))


def _in_proj(x, gn, w, bias, gq, gsel, gwin, gfq, gfk, seg, tm, seq=None):
    n = x.shape[0]
    row = lambda width: pl.BlockSpec((tm, width), lambda i: (i, 0))
    const = lambda a: pl.BlockSpec(a.shape, lambda i: (0, 0))
    consts = (gn, w, bias, gq, gsel, gwin, gfq, gfk, seg)
    out_specs, out_shape = [], []
    for width, dt, is_state in IN_PROJ_OUT:
        if is_state and seq is not None:
            per = seq // tm
            out_specs.append(pl.BlockSpec((1, width, tm), lambda i: (i // per, 0, i % per)))
            out_shape.append(jax.ShapeDtypeStruct((n // seq, width, seq), dt))
        else:
            out_specs.append(row(width))
            out_shape.append(jax.ShapeDtypeStruct((n, width), dt))
    return pl.pallas_call(
        _in_proj_kernel,
        grid=(n // tm,),
        in_specs=[row(D_MODEL)] + [const(a) for a in consts],
        out_specs=out_specs,
        out_shape=out_shape,
        compiler_params=_cparams(("parallel",)),
        name="in_proj",
    )(x, *consts)


def _compress_kernel(x_ref, w_ref, blk_ref, gk_ref, seg_ref, kc_ref, vc_ref):
    hi, mid, lo = _split3(x_ref[0] * w_ref[...])
    blk = blk_ref[...]
    c = _dot_nt(blk, hi) + _dot_nt(blk, mid) + _dot_nt(blk, lo)
    kc_ref[...] = _head_rms(c[:, :LANES], seg_ref[...], gk_ref[...]).astype(BF16)
    vc_ref[...] = c[:, LANES:].astype(BF16)


def _compress(kv_cmp_t, w_rows, gk, seg1, tr):
    batch, _, seq = kv_cmp_t.shape
    nb = tr // CMP_BLOCK
    per = seq // tr
    w_t = jnp.tile(w_rows.T, (1, nb))
    blk = (jnp.arange(tr)[None, :] // CMP_BLOCK == jnp.arange(nb)[:, None]).astype(BF16)
    out = jax.ShapeDtypeStruct((batch * seq // CMP_BLOCK, LANES), BF16)
    const = lambda a: pl.BlockSpec(a.shape, lambda i: (0, 0))
    return pl.pallas_call(
        _compress_kernel,
        grid=(batch * per,),
        in_specs=[pl.BlockSpec((1, 256, tr), lambda i: (i // per, 0, i % per)),
                  const(w_t), const(blk), const(gk), const(seg1)],
        out_specs=[pl.BlockSpec((nb, LANES), lambda i: (i, 0))] * 2,
        out_shape=[out, out],
        compiler_params=_cparams(("parallel",)),
        name="compress",
    )(kv_cmp_t, w_t, blk, gk, seg1)


TQ_NSA = 128
TK_SEL = 512
SEL_OFF = 16384.0


def _flash_head(s, r0, m_ref, l_ref, acc_ref, p_ref):
    rows, tk = s.shape
    rs = slice(r0, r0 + rows)
    m_old = m_ref[rs]
    m_new = jnp.maximum(m_old, jnp.max(s, axis=1, keepdims=True))
    psum = jnp.zeros((rows, LANES), F32)
    for c in range(tk // LANES):
        cs = slice(c * LANES, (c + 1) * LANES)
        p = jnp.exp(s[:, cs] - m_new)
        p_ref[rs, cs] = p.astype(BF16)
        psum = psum + p
    alpha = jnp.exp(m_old - m_new)
    l_ref[rs] = alpha * l_ref[rs] + jnp.sum(psum, axis=1, keepdims=True)
    for c in range(acc_ref.shape[1] // LANES):
        cs = slice(c * LANES, (c + 1) * LANES)
        acc_ref[rs, cs] = alpha * acc_ref[rs, cs]
    m_ref[rs] = m_new


def _softmax_finish(m_ref, l_ref, acc_ref):
    inv = 1.0 / jnp.maximum(l_ref[...], 1e-30)
    return jnp.concatenate([acc_ref[:, c * LANES:(c + 1) * LANES] * inv for c in range(acc_ref.shape[1] // LANES)],
                           axis=1)


def _softmax_reset(m_ref, l_ref, acc_ref):
    m_ref[...] = jnp.full(m_ref.shape, NEG, F32)
    l_ref[...] = jnp.zeros(l_ref.shape, F32)
    acc_ref[...] = jnp.zeros(acc_ref.shape, F32)


def _nsa_slope(h):
    return 2.0 ** (-(h + 1))


def _nsa_prompt_kernel(q_ref, gate_ref, kc_ref, vc_ref, ks_ref, vs_ref, kw_ref, vw_ref, pairt_ref, exp_ref,
                       o_ref, q8_ref, m_ref, l_ref, acc_ref, p_ref, oc_ref, seln_ref, used_ref, impc_ref):
    i = pl.program_id(1)
    tq = TQ_NSA
    q0 = i * tq
    nbc = kc_ref.shape[1]
    nbs = nbc // 2
    low = _lane_lt64((tq, LANES))
    for g in range(KVH_NSA):
        for r in range(NSA_REP):
            qt = q_ref[:, r * LANES:(r + 1) * LANES]
            keep = low if g == 0 else jnp.logical_not(low)
            q8_ref[(g * NSA_REP + r) * tq:(g * NSA_REP + r + 1) * tq, :] = jnp.where(keep, qt, jnp.zeros_like(qt))
    qpos = q0 + lax.broadcasted_iota(jnp.int32, (tq, 1), 0)

    s_all = _dot_nt(q8_ref[...], kc_ref[0])
    blk = lax.broadcasted_iota(jnp.int32, (tq, nbc), 1)
    dc = qpos - (blk * CMP_BLOCK + (CMP_BLOCK - 1))
    vis = dc >= 0
    dcf = dc.astype(F32)
    lane = lax.broadcasted_iota(jnp.int32, (tq, LANES), 1)
    visible = (lane * SEL_BLOCK <= qpos) & (lane < nbs)
    for g in range(KVH_NSA):
        imp_c = jnp.zeros((tq, nbc), F32)
        for r in range(NSA_REP):
            h = g * NSA_REP + r
            s = jnp.where(vis, s_all[h * tq:(h + 1) * tq] - _nsa_slope(h) * dcf, NEG)
            m = jnp.max(s, axis=1, keepdims=True)
            p = jnp.where(vis, jnp.exp(s - m), 0.0)
            p = p / jnp.maximum(jnp.sum(p, axis=1, keepdims=True), 1e-30)
            imp_c = imp_c + p
            p_ref[h * tq:(h + 1) * tq, :nbc] = p.astype(BF16)
        impc_ref[g] = imp_c
    oc_ref[...] = _dot(p_ref[:, :nbc], vc_ref[0])

    few = q0 + tq <= SEL_TOPK * SEL_BLOCK

    @pl.when(few)
    def _():
        keep = jnp.where(visible, 0.0, -SEL_OFF).astype(BF16)
        for g in range(KVH_NSA):
            seln_ref[g] = keep

    @pl.when(jnp.logical_not(few))
    def _():
        imp_t = []
        for g in range(KVH_NSA):
            hi, lo = _split2(impc_ref[g])
            imp_t.append(_dot_nt(pairt_ref[...], hi) + _dot_nt(pairt_ref[...], lo))
        imp_t = jnp.concatenate(imp_t, axis=1)
        shape = imp_t.shape
        blk_j = lax.broadcasted_iota(jnp.int32, shape, 0)
        qp = q0 + jnp.bitwise_and(lax.broadcasted_iota(jnp.int32, shape, 1), tq - 1)
        cur = jnp.right_shift(qp, 6)
        forced = (blk_j == 0) | (blk_j == cur) | (blk_j == cur - 1)
        score = jnp.where((blk_j * SEL_BLOCK <= qp) & (blk_j < nbs),
                          imp_t + jnp.where(forced, FORCE_BONUS, 0.0), NEG)
        blk_f = blk_j.astype(F32)
        sel = jnp.full(shape, -SEL_OFF, F32)
        for _ in range(min(SEL_TOPK, nbs)):
            best = jnp.max(score, axis=0, keepdims=True)
            first = jnp.min(jnp.where(score == best, blk_f, 1e9), axis=0, keepdims=True)
            hit = blk_f == first
            sel = jnp.where(hit, 0.0, sel)
            score = jnp.where(hit, NEG, score)
        for g in range(KVH_NSA):
            seln_ref[g] = sel[:, g * tq:(g + 1) * tq].T.astype(BF16)

    def branch_out():
        return _softmax_finish(m_ref, l_ref, acc_ref)

    _softmax_reset(m_ref, l_ref, acc_ref)

    used = jnp.zeros((1, LANES), F32)
    for g in range(KVH_NSA):
        used = used + (jnp.max(seln_ref[g].astype(F32), axis=0, keepdims=True) + SEL_OFF) * ((g + 1) / SEL_OFF)
    used_ref[0:1, :] = used
    blk_tile = jnp.right_shift(lax.broadcasted_iota(jnp.int32, (1, LANES), 1),
                               (TK_SEL // SEL_BLOCK).bit_length() - 1)

    def sel_tile(kt, carry):
        k0 = pl.multiple_of(kt * TK_SEL, TK_SEL)
        u = jnp.where(blk_tile == kt, used_ref[0:1, :], 0.0)
        picked = (jnp.max(jnp.where(u == 2.0, 0.0, u)).astype(jnp.int32) & 1,
                  jnp.max(jnp.where(u == 1.0, 0.0, u)).astype(jnp.int32) >> 1)
        dist = qpos - (k0 + lax.broadcasted_iota(jnp.int32, (tq, TK_SEL), 1))
        distf = dist.astype(F32)
        for g in range(KVH_NSA):
            rows = slice(g * NSA_REP * tq, (g + 1) * NSA_REP * tq)

            @pl.when(picked[g] > 0)
            def _():
                s_g = _dot_nt(q8_ref[rows, :], ks_ref[pl.ds(k0, TK_SEL), :])
                bias = jnp.where(dist >= 0, _dot(seln_ref[g], exp_ref[kt]), NEG)
                for r in range(NSA_REP):
                    h = g * NSA_REP + r
                    _flash_head(s_g[r * tq:(r + 1) * tq] - _nsa_slope(h) * distf + bias, h * tq,
                                m_ref, l_ref, acc_ref, p_ref)
                acc_ref[rows, :] += _dot(p_ref[rows, :TK_SEL], vs_ref[pl.ds(k0, TK_SEL), :])
        return carry

    lax.fori_loop(0, (q0 + tq - 1) // TK_SEL + 1, sel_tile, 0)
    o_sel = branch_out()

    wk = WINDOW + tq
    w0 = pl.multiple_of(jnp.maximum(q0 - WINDOW, 0), tq)
    s_all = _dot_nt(q8_ref[...], kw_ref[pl.ds(w0, wk), :])
    dist = qpos - (w0 + lax.broadcasted_iota(jnp.int32, (tq, wk), 1))
    distf = dist.astype(F32)
    bias = jnp.where(jnp.where(dist >= 0, dist, WINDOW + 1) <= WINDOW, 0.0, NEG)
    for h in range(H_NSA):
        rows = slice(h * tq, (h + 1) * tq)
        s = s_all[rows] - _nsa_slope(h) * distf + bias
        p = jnp.exp(s - jnp.max(s, axis=1, keepdims=True))
        l_ref[rows] = jnp.broadcast_to(jnp.sum(p, axis=1, keepdims=True), (tq, LANES))
        p_ref[rows, :wk] = p.astype(BF16)
    o_win = _dot(p_ref[:, :wk], vw_ref[pl.ds(w0, wk), :]) / l_ref[...]

    gates = gate_ref[...]
    oc = oc_ref[...]
    for r in range(NSA_REP):
        halves = []
        for g in range(KVH_NSA):
            h = g * NSA_REP + r
            rows = slice(h * tq, (h + 1) * tq)
            halves.append(gates[:, 3 * h:3 * h + 1] * oc[rows] + gates[:, 3 * h + 1:3 * h + 2] * o_sel[rows]
                          + gates[:, 3 * h + 2:3 * h + 3] * o_win[rows])
        o_ref[:, r * LANES:(r + 1) * LANES] = jnp.where(low, halves[0], halves[1])


def _nsa_prompt(q, misc, kc, vc, selb, winb, pair, expand, batch, seq):
    nq = seq // TQ_NSA
    nbc = seq // CMP_BLOCK
    rows = H_NSA * TQ_NSA
    qmap = lambda b, i: (b * nq + i, 0)
    return pl.pallas_call(
        _nsa_prompt_kernel,
        grid=(batch, nq),
        in_specs=[pl.BlockSpec((TQ_NSA, 512), qmap),
                  pl.BlockSpec((TQ_NSA, LANES), qmap),
                  pl.BlockSpec((1, nbc, LANES), lambda b, i: (b, 0, 0)),
                  pl.BlockSpec((1, nbc, LANES), lambda b, i: (b, 0, 0)),
                  pl.BlockSpec((seq, LANES), lambda b, i: (b, 0)),
                  pl.BlockSpec((seq, LANES), lambda b, i: (b, 1)),
                  pl.BlockSpec((seq, LANES), lambda b, i: (b, 0)),
                  pl.BlockSpec((seq, LANES), lambda b, i: (b, 1)),
                  pl.BlockSpec(pair.shape, lambda b, i: (0, 0)),
                  pl.BlockSpec(expand.shape, lambda b, i: (0, 0, 0))],
        out_specs=pl.BlockSpec((TQ_NSA, 512), qmap),
        out_shape=jax.ShapeDtypeStruct((batch * seq, 512), F32),
        scratch_shapes=[pltpu.VMEM((rows, LANES), BF16),
                        pltpu.VMEM((rows, LANES), F32), pltpu.VMEM((rows, LANES), F32),
                        pltpu.VMEM((rows, LANES), F32),
                        pltpu.VMEM((rows, max(TK_SEL, WINDOW + TQ_NSA)), BF16),
                        pltpu.VMEM((rows, LANES), F32),
                        pltpu.VMEM((KVH_NSA, TQ_NSA, LANES), BF16),
                        pltpu.VMEM((8, LANES), F32),
                        pltpu.VMEM((KVH_NSA, TQ_NSA, nbc), F32)],
        compiler_params=_cparams(("parallel", "arbitrary")),
        name="nsa_prompt",
    )(q, misc, kc, vc, selb, selb, winb, winb, pair, expand)


TQ_ATT = 256


def _lane_head(shape):
    return jnp.right_shift(lax.broadcasted_iota(jnp.int32, shape, len(shape) - 1), 6)


def _stack_heads(q, q4_ref, heads):
    tq = q.shape[0]
    head = _lane_head(q.shape)
    for h in range(heads):
        q4_ref[h * tq:(h + 1) * tq, :] = jnp.where(head == h, q, jnp.zeros_like(q)).astype(q4_ref.dtype)


def _unstack_heads(acc, heads):
    tq = acc.shape[0] // heads
    head = _lane_head((tq, acc.shape[1]))
    out = acc[0:tq]
    for h in range(1, heads):
        out = jnp.where(head == h, acc[h * tq:(h + 1) * tq], out)
    return out


def _sb_prompt_kernel(q_ref, k_ref, v_ref, u_ref, o_ref, q4_ref, carry_ref, acc_ref, p_ref):
    i = pl.program_id(1)
    tq = TQ_ATT
    _stack_heads(q_ref[...], q4_ref, H_SB)
    carry_ref[...] = jnp.zeros(carry_ref.shape, F32)
    acc_ref[...] = jnp.zeros(acc_ref.shape, F32)
    qpos = i * tq + lax.broadcasted_iota(jnp.int32, (tq, 1), 0)

    def tile(j):
        k0 = pl.multiple_of((i - j) * tq, tq)
        z_all = _dot_nt(q4_ref[...], k_ref[pl.ds(k0, tq), :])
        mask = (k0 + lax.broadcasted_iota(jnp.int32, (tq, tq), 1)) < qpos
        for h in range(H_SB):
            rows = slice(h * tq, (h + 1) * tq)
            z = z_all[rows]
            sp = jnp.log(1.0 + jnp.exp(-jnp.abs(z)))
            lk = jnp.where(mask, -(jnp.maximum(z, 0.0) + sp), 0.0)
            e = jnp.minimum(z, 0.0) - sp + _dot_f32(lk, u_ref[...])
            car = carry_ref[rows]
            for cc in range(tq // LANES):
                cs = slice(cc * LANES, (cc + 1) * LANES)
                p_ref[rows, cs] = jnp.where(mask[:, cs], jnp.exp(e[:, cs] + car), 0.0).astype(BF16)
            carry_ref[rows] = car + jnp.sum(lk, axis=1, keepdims=True)
        acc_ref[...] += _dot(p_ref[...], v_ref[pl.ds(k0, tq), :])
        return jnp.max(carry_ref[...])

    lax.while_loop(lambda s: (s[0] <= i) & (s[1] > EXP_UNDERFLOW), lambda s: (s[0] + 1, tile(s[0])),
                   (jnp.int32(0), jnp.float32(0.0)))
    o_ref[...] = _unstack_heads(acc_ref[...], H_SB)


def _sb_prompt(q, kvb, upper, batch, seq):
    nq = seq // TQ_ATT
    rows = H_SB * TQ_ATT
    qmap = lambda b, i: (b * nq + i, 0)
    return pl.pallas_call(
        _sb_prompt_kernel,
        grid=(batch, nq),
        in_specs=[pl.BlockSpec((TQ_ATT, 256), qmap),
                  pl.BlockSpec((seq, 256), lambda b, i: (b, 0)),
                  pl.BlockSpec((seq, 256), lambda b, i: (b, 1)),
                  pl.BlockSpec(upper.shape, lambda b, i: (0, 0))],
        out_specs=pl.BlockSpec((TQ_ATT, 256), qmap),
        out_shape=jax.ShapeDtypeStruct((batch * seq, 256), F32),
        scratch_shapes=[pltpu.VMEM((rows, 256), BF16), pltpu.VMEM((rows, LANES), F32),
                        pltpu.VMEM((rows, 256), F32), pltpu.VMEM((rows, TQ_ATT), BF16)],
        compiler_params=_cparams(("parallel", "arbitrary")),
        name="sb_prompt",
    )(q, kvb, kvb, upper)


def _fox_prompt_kernel(q_ref, k_ref, v_ref, cq_ref, ck_ref, o_ref, q4_ref, m_ref, l_ref, acc_ref, p_ref, cqb_ref):
    i = pl.program_id(1)
    tq = TQ_ATT
    _stack_heads(q_ref[...], q4_ref, H_FOX)
    _softmax_reset(m_ref, l_ref, acc_ref)
    qpos = i * tq + lax.broadcasted_iota(jnp.int32, (tq, 1), 0)
    cq = cq_ref[...]
    for h in range(H_FOX):
        cqb_ref[h * tq:(h + 1) * tq] = jnp.broadcast_to(cq[:, GATE_COLS + h:GATE_COLS + h + 1], (tq, LANES))

    def tile(kt, c):
        k0 = pl.multiple_of(kt * tq, tq)
        s_all = _dot_nt(q4_ref[...], k_ref[pl.ds(k0, tq), :])
        mask = (k0 + lax.broadcasted_iota(jnp.int32, (tq, tq), 1)) <= qpos
        ck = ck_ref[0, kt]
        for h in range(H_FOX):
            rows = slice(h * tq, (h + 1) * tq)
            cqh = cqb_ref[rows]
            decay = jnp.concatenate([cqh - ck[h:h + 1, cc * LANES:(cc + 1) * LANES] for cc in range(tq // LANES)],
                                    axis=1)
            _flash_head(jnp.where(mask, s_all[rows] + decay, NEG), h * tq, m_ref, l_ref, acc_ref, p_ref)
        acc_ref[...] += _dot(p_ref[...], v_ref[pl.ds(k0, tq), :])
        return c

    lax.fori_loop(0, i + 1, tile, 0)
    o_ref[...] = _unstack_heads(_softmax_finish(m_ref, l_ref, acc_ref), H_FOX)


def _fox_prompt(q, kvb, c_rows, c_lanes, batch, seq):
    nq = seq // TQ_ATT
    rows = H_FOX * TQ_ATT
    qmap = lambda b, i: (b * nq + i, 0)
    return pl.pallas_call(
        _fox_prompt_kernel,
        grid=(batch, nq),
        in_specs=[pl.BlockSpec((TQ_ATT, 256), qmap),
                  pl.BlockSpec((seq, 256), lambda b, i: (b, 0)),
                  pl.BlockSpec((seq, 256), lambda b, i: (b, 1)),
                  pl.BlockSpec((TQ_ATT, LANES), qmap),
                  pl.BlockSpec((1, nq, 8, TQ_ATT), lambda b, i: (b, 0, 0, 0))],
        out_specs=pl.BlockSpec((TQ_ATT, 256), qmap),
        out_shape=jax.ShapeDtypeStruct((batch * seq, 256), F32),
        scratch_shapes=[pltpu.VMEM((rows, 256), BF16), pltpu.VMEM((rows, LANES), F32), pltpu.VMEM((rows, LANES), F32),
                        pltpu.VMEM((rows, 256), F32), pltpu.VMEM((rows, TQ_ATT), BF16),
                        pltpu.VMEM((rows, LANES), F32)],
        compiler_params=_cparams(("parallel", "arbitrary")),
        name="fox_prompt",
    )(q, kvb, kvb, c_rows, c_lanes)


CUMSUM_CHUNK = 512


def _cumsum_kernel(x_ref, tri_ref, o_ref):
    chunk = tri_ref.shape[0]
    carry = jnp.zeros((1, LANES), F32)
    for c in range(x_ref.shape[0] // chunk):
        rows = slice(c * chunk, (c + 1) * chunk)
        y = _dot_f32_left(tri_ref[...], x_ref[rows, :]) + carry
        o_ref[rows, :] = y
        carry = y[chunk - 1:chunk, :]


def _cumsum_rows(x, tri, batch, seq):
    return pl.pallas_call(
        _cumsum_kernel,
        grid=(batch,),
        in_specs=[pl.BlockSpec((seq, LANES), lambda b: (b, 0)), pl.BlockSpec(tri.shape, lambda b: (0, 0))],
        out_specs=pl.BlockSpec((seq, LANES), lambda b: (b, 0)),
        out_shape=jax.ShapeDtypeStruct(x.shape, F32),
        compiler_params=_cparams(("parallel",)),
        name="cumsum_rows",
    )(x, tri)


def _out_proj_kernel(on_ref, osb_ref, ofox_ref, g_ref, w_ref, x_ref, seg_ref, y_ref):
    seg = seg_ref[...]
    acc = x_ref[...]
    for ref, c0, width in ((on_ref, 0, 512), (osb_ref, 512, 256), (ofox_ref, 768, 256)):
        for j in range(width // 256):
            c = c0 + 256 * j
            o = _head_rms(ref[:, 256 * j:256 * (j + 1)], seg, g_ref[:, c:c + 256])
            acc = acc + _dot(o.astype(BF16), w_ref[c:c + 256, :])
    y_ref[...] = acc


def _out_proj(o_nsa, o_sb, o_fox, g, w, x, seg, tm):
    n = x.shape[0]
    row = lambda width: pl.BlockSpec((tm, width), lambda i: (i, 0))
    const = lambda a: pl.BlockSpec(a.shape, lambda i: (0, 0))
    return pl.pallas_call(
        _out_proj_kernel,
        grid=(n // tm,),
        in_specs=[row(512), row(256), row(256), const(g), const(w), row(D_MODEL), const(seg)],
        out_specs=row(D_MODEL),
        out_shape=jax.ShapeDtypeStruct(x.shape, F32),
        compiler_params=_cparams(("parallel",)),
        name="out_proj",
    )(o_nsa, o_sb, o_fox, g, w, x, seg)


def _mem_kv_kernel(x_ref, w_ref, gk_ref, seg_ref, kv_ref, kvb_ref):
    kv = _dot(x_ref[...].astype(BF16), w_ref[...])
    k = _head_rms(kv[:, :256], seg_ref[...], gk_ref[...])
    kv_ref[:, :256] = k
    kv_ref[:, 256:] = kv[:, 256:]
    kvb_ref[:, :256] = k.astype(BF16)
    kvb_ref[:, 256:] = kv[:, 256:].astype(BF16)


def _mem_kv(mem, w, gk, seg, tm):
    n = mem.shape[0]
    row = lambda width: pl.BlockSpec((tm, width), lambda i: (i, 0))
    const = lambda a: pl.BlockSpec(a.shape, lambda i: (0, 0))
    return pl.pallas_call(
        _mem_kv_kernel,
        grid=(n // tm,),
        in_specs=[row(D_MODEL), const(w), const(gk), const(seg)],
        out_specs=[row(512), row(512)],
        out_shape=[jax.ShapeDtypeStruct((n, 512), F32), jax.ShapeDtypeStruct((n, 512), BF16)],
        compiler_params=_cparams(("parallel",)),
        name="mem_kv",
    )(mem, w, gk, seg)


def _mem_q_kernel(x_ref, gn_ref, w_ref, gq_ref, seg_ref, q_ref):
    xn = _row_rms(x_ref[...], gn_ref[...]).astype(BF16)
    q = _dot(xn, w_ref[...])
    q_ref[...] = (_head_rms(q, seg_ref[...], gq_ref[...]) * SCALE).astype(q_ref.dtype)


def _mem_q(x, gn, w, gq, seg, tm, dtype):
    n = x.shape[0]
    row = lambda width: pl.BlockSpec((tm, width), lambda i: (i, 0))
    const = lambda a: pl.BlockSpec(a.shape, lambda i: (0, 0))
    return pl.pallas_call(
        _mem_q_kernel,
        grid=(n // tm,),
        in_specs=[row(D_MODEL), const(gn), const(w), const(gq), const(seg)],
        out_specs=row(256),
        out_shape=jax.ShapeDtypeStruct((n, 256), dtype),
        compiler_params=_cparams(("parallel",)),
        name="mem_q",
    )(x, gn, w, gq, seg)


def _mem_attn_kernel(q_ref, kv_ref, w_ref, x_ref, y_ref, q4_ref):
    _stack_heads(q_ref[0], q4_ref, H_MEM)
    s = _dot_nt(q4_ref[...].astype(BF16), kv_ref[0, 0, :, :256].astype(BF16))
    p = jnp.exp(s - jnp.max(s, axis=1, keepdims=True))
    p = p / jnp.sum(p, axis=1, keepdims=True)
    o = _unstack_heads(_dot(p.astype(BF16), kv_ref[0, 0, :, 256:].astype(BF16)), H_MEM)
    y_ref[0] = x_ref[0] + _dot(o.astype(BF16), w_ref[...])


def _mem_attn(q, kv, layer, w, x, tq):
    batch, seq, _ = x.shape
    return pl.pallas_call(
        _mem_attn_kernel,
        grid=(batch, seq // tq),
        in_specs=[pl.BlockSpec((1, tq, 256), lambda b, i: (b, i, 0)),
                  pl.BlockSpec((1, 1, N_MEM, 512), lambda b, i: (layer, b, 0, 0)),
                  pl.BlockSpec(w.shape, lambda b, i: (0, 0)),
                  pl.BlockSpec((1, tq, D_MODEL), lambda b, i: (b, i, 0))],
        out_specs=pl.BlockSpec((1, tq, D_MODEL), lambda b, i: (b, i, 0)),
        out_shape=jax.ShapeDtypeStruct(x.shape, F32),
        scratch_shapes=[pltpu.VMEM((H_MEM * tq, 256), q.dtype)],
        compiler_params=_cparams(("parallel", "parallel")),
        name="mem_attn",
    )(q, kv, w, x)


FF_CHUNK = 1024


def _mlp_kernel(x_ref, g_ref, wu_ref, wd_ref, y_ref, xn_ref):
    k = pl.program_id(1)

    @pl.when(k == 0)
    def _():
        x = x_ref[...]
        xn_ref[...] = _row_rms(x, g_ref[...]).astype(BF16)
        y_ref[...] = x

    h = jnp.maximum(_dot(xn_ref[...], wu_ref[...]), 0.0)
    y_ref[...] += _dot((h * h).astype(BF16), wd_ref[...])


def _mlp(x, g, w_up, w_down, tm):
    n = x.shape[0]
    return pl.pallas_call(
        _mlp_kernel,
        grid=(n // tm, D_FF // FF_CHUNK),
        in_specs=[pl.BlockSpec((tm, D_MODEL), lambda i, k: (i, 0)),
                  pl.BlockSpec(g.shape, lambda i, k: (0, 0)),
                  pl.BlockSpec((D_MODEL, FF_CHUNK), lambda i, k: (0, k)),
                  pl.BlockSpec((FF_CHUNK, D_MODEL), lambda i, k: (k, 0))],
        out_specs=pl.BlockSpec((tm, D_MODEL), lambda i, k: (i, 0)),
        out_shape=jax.ShapeDtypeStruct(x.shape, F32),
        scratch_shapes=[pltpu.VMEM((tm, D_MODEL), BF16)],
        compiler_params=_cparams(("parallel", "arbitrary")),
        name="mlp",
    )(x, g, w_up, w_down)


def _dot_f32x3(x, m):
    hi, mid, lo = _split3(x)
    return _dot(hi, m) + _dot(mid, m) + _dot(lo, m)


def _head_rows(q_row, rows=8):
    shape = (rows, q_row.shape[1])
    row = lax.broadcasted_iota(jnp.int32, shape, 0)
    return jnp.where(_lane_head(shape) == row, jnp.broadcast_to(q_row, shape), 0.0)


def _unstack_rows(acc, heads):
    head = _lane_head((1, acc.shape[1]))
    out = acc[0:1]
    for h in range(1, heads):
        out = jnp.where(head == h, acc[h:h + 1], out)
    return out


PAGES_PER_STEP = 8


def _decode_pages_kernel(pt_ref, *refs):
    del pt_ref
    n = PAGES_PER_STEP
    cmp_refs, sb_refs, fox_refs, lf_refs = refs[0:n], refs[n:2 * n], refs[2 * n:3 * n], refs[3 * n:4 * n]
    (qsb_ref, qfox_ref, foxnew_ref, lfnew_ref, wcmp_ref, u_ref, blk_ref, kc_ref, osb_ref, ofox_ref,
     qs8_ref, qf8_ref, car_ref, accs_ref, m_ref, l_ref, accf_ref, carf_ref) = refs[4 * n:]
    p = pl.program_id(1)
    last = pl.num_programs(1) - 1
    rep = lambda col: jnp.broadcast_to(col, (8, LANES))

    @pl.when(p == 0)
    def _():
        qs8_ref[...] = _head_rows(qsb_ref[0])
        qf8 = _head_rows(qfox_ref[0])
        qf8_ref[...] = qf8
        car_ref[...] = jnp.zeros(car_ref.shape, F32)
        accs_ref[...] = jnp.zeros(accs_ref.shape, F32)
        new = foxnew_ref[0]
        m_ref[...] = rep(jnp.sum(qf8 * new[:, :256], axis=1, keepdims=True))
        l_ref[...] = jnp.ones(l_ref.shape, F32)
        accf_ref[...] = jnp.broadcast_to(new[:, 256:], accf_ref.shape)
        carf_ref[...] = lfnew_ref[0]

    u = u_ref[...]
    qs8 = qs8_ref[...].astype(BF16)
    qf8 = qf8_ref[...].astype(BF16)
    @pl.when(jnp.max(car_ref[0:H_SB, :]) > EXP_UNDERFLOW)
    def _():
        car = car_ref[...]
        acc_s = accs_ref[...]
        e, tot = [], []
        for j in range(n):
            z = _dot(qs8, sb_refs[j][0, 0, :256, :].astype(BF16))
            sp = jnp.log(1.0 + jnp.exp(-jnp.abs(z)))
            lk = -(jnp.maximum(z, 0.0) + sp)
            e.append(jnp.minimum(z, 0.0) - sp + _dot_f32(lk, u))
            tot.append(jnp.sum(lk, axis=1, keepdims=True))
        for j in range(n):
            a = jnp.exp(e[j] + car)
            acc_s = acc_s + _dot_nt(a.astype(BF16), sb_refs[j][0, 0, 256:, :].astype(BF16))
            car = car + tot[j]
        car_ref[...] = car
        accs_ref[...] = acc_s

    s_loc, tot_f = [], []
    for j in range(n):
        lf = lf_refs[j][0, 0]
        s_loc.append(_dot(qf8, fox_refs[j][0, 0, :256, :].astype(BF16)) + _dot_f32x3(lf, u))
        tot_f.append(jnp.sum(lf, axis=1, keepdims=True))
        hi, mid, lo = _split3(cmp_refs[j][0, 0] * wcmp_ref[...])
        blk = blk_ref[...]
        c = _dot_nt(blk, hi) + _dot_nt(blk, mid) + _dot_nt(blk, lo)
        kc_ref[0, (last - p) * n + (n - 1 - j)] = c[0:kc_ref.shape[2]]

    carf = carf_ref[...]
    s = []
    for j in range(n):
        s.append(s_loc[j] + carf)
        carf = carf + tot_f[j]
    carf_ref[...] = carf

    m_old = m_ref[...]
    m_new = m_old
    for j in range(n):
        m_new = jnp.maximum(m_new, jnp.max(s[j], axis=1, keepdims=True))
    alpha = jnp.exp(m_old - m_new)
    l_new = alpha * l_ref[...]
    acc_f = jnp.concatenate([alpha, alpha], axis=1) * accf_ref[...]
    for j in range(n):
        pe = jnp.exp(s[j] - m_new)
        l_new = l_new + jnp.sum(pe, axis=1, keepdims=True)
        acc_f = acc_f + _dot_nt(pe.astype(BF16), fox_refs[j][0, 0, 256:, :].astype(BF16))
    l_ref[...] = l_new
    accf_ref[...] = acc_f
    m_ref[...] = m_new

    @pl.when(p == last)
    def _():
        osb_ref[0] = _unstack_rows(accs_ref[...], H_SB)
        inv = 1.0 / l_ref[...]
        ofox_ref[0] = _unstack_rows(accf_ref[...] * jnp.concatenate([inv, inv], axis=1), H_FOX)


def _decode_pages(pt_flat, layer, cmp_pool, sb_pool, fox_pool, lf_pool, qsb, qfox, foxnew, lfnew, wcmp, upper,
                  nseq, npages):
    page = cmp_pool.shape[3]
    per_page = page // CMP_BLOCK
    n = PAGES_PER_STEP
    steps = npages // n
    blk_rows = (jnp.arange(page)[None, :] // CMP_BLOCK == jnp.arange(8)[:, None]).astype(BF16)

    def pmap(j):
        return lambda b, p, pt: (layer, pt[jnp.minimum(b, nseq - 1) * npages + (steps - 1 - p) * n + (n - 1 - j)],
                                 0, 0)

    smap = lambda b, p, pt: (b, 0, 0)
    cmap = lambda b, p, pt: (0, 0)
    paged = lambda width: [pl.BlockSpec((1, 1, width, page), pmap(j)) for j in range(n)]
    grid_spec = pltpu.PrefetchScalarGridSpec(
        num_scalar_prefetch=1,
        grid=(nseq, steps),
        in_specs=paged(256) + paged(512) + paged(512) + paged(8)
        + [pl.BlockSpec((1, 1, 256), smap),
           pl.BlockSpec((1, 1, 256), smap),
           pl.BlockSpec((1, 1, 512), smap),
           pl.BlockSpec((1, 8, LANES), smap),
           pl.BlockSpec(wcmp.shape, cmap),
           pl.BlockSpec(upper.shape, cmap),
           pl.BlockSpec(blk_rows.shape, cmap)],
        out_specs=[pl.BlockSpec((1, npages, per_page, 256), lambda b, p, pt: (b, 0, 0, 0)),
                   pl.BlockSpec((1, 1, 256), smap),
                   pl.BlockSpec((1, 1, 256), smap)],
        scratch_shapes=[pltpu.VMEM((8, 256), F32), pltpu.VMEM((8, 256), F32), pltpu.VMEM((8, LANES), F32),
                        pltpu.VMEM((8, 256), F32), pltpu.VMEM((8, LANES), F32), pltpu.VMEM((8, LANES), F32),
                        pltpu.VMEM((8, 256), F32), pltpu.VMEM((8, LANES), F32)],
    )
    pools = [cmp_pool] * n + [sb_pool] * n + [fox_pool] * n + [lf_pool] * n
    return pl.pallas_call(
        _decode_pages_kernel,
        grid_spec=grid_spec,
        out_shape=[jax.ShapeDtypeStruct((nseq, npages, per_page, 256), F32),
                   jax.ShapeDtypeStruct((nseq, 1, 256), F32),
                   jax.ShapeDtypeStruct((nseq, 1, 256), F32)],
        compiler_params=_cparams(("parallel", "arbitrary")),
        name="decode_pages",
    )(pt_flat, *pools, qsb, qfox, foxnew, lfnew, wcmp, upper, blk_rows)


def _nsa_q8(q_row):
    low = _lane_lt64((1, LANES))
    rows = []
    for g in range(KVH_NSA):
        for r in range(NSA_REP):
            t = q_row[:, r * LANES:(r + 1) * LANES]
            rows.append(jnp.where(low if g == 0 else jnp.logical_not(low), t, 0.0))
    return jnp.concatenate(rows, axis=0)


def _slope_col():
    row = lax.broadcasted_iota(jnp.int32, (H_NSA, 1), 0)
    out = jnp.zeros((H_NSA, 1), F32)
    for h in range(H_NSA):
        out = jnp.where(row == h, _nsa_slope(h), out)
    return out


def _decode_cmp_kernel(kc_ref, q_ref, gk_ref, seg_ref, pair_ref, oc_ref, idx_ref, *, past_len):
    kcr = kc_ref[0]
    nbc = kcr.shape[0]
    k = _head_rms(kcr[:, :LANES], seg_ref[...], gk_ref[...]).astype(BF16)
    q8 = _nsa_q8(q_ref[0])
    slope = _slope_col()
    blk = lax.broadcasted_iota(jnp.int32, (1, nbc), 1)
    dc = past_len - (blk * CMP_BLOCK + (CMP_BLOCK - 1))
    vis = dc >= 0
    s = jnp.where(vis, _dot_nt(q8.astype(BF16), k) - slope * dc.astype(F32), NEG)
    m = jnp.max(s, axis=1, keepdims=True)
    p = jnp.where(vis, jnp.exp(s - m), 0.0)
    p = p / jnp.maximum(jnp.sum(p, axis=1, keepdims=True), 1e-30)
    oc_ref[0] = _dot(p.astype(BF16), kcr[:, LANES:].astype(BF16))
    row = lax.broadcasted_iota(jnp.int32, (8, nbc), 0)
    imp_c = jnp.zeros((8, nbc), F32)
    for g in range(KVH_NSA):
        tot = jnp.sum(p[g * NSA_REP:(g + 1) * NSA_REP], axis=0, keepdims=True)
        imp_c = jnp.where(row == g, jnp.broadcast_to(tot, (8, nbc)), imp_c)
    imp = _dot_f32(imp_c, pair_ref[...])
    lane = lax.broadcasted_iota(jnp.int32, (8, LANES), 1)
    cur = past_len // SEL_BLOCK
    nbs = past_len // SEL_BLOCK
    forced = (lane == 0) | (lane == cur - 1)
    score = jnp.where(lane < nbs, imp + jnp.where(forced, FORCE_BONUS, 0.0), NEG)
    lanef = lane.astype(F32)
    picks = jnp.zeros((8, LANES), F32)
    for it in range(SEL_TOPK - 1):
        best = jnp.max(score, axis=1, keepdims=True)
        first = jnp.min(jnp.where(score == best, lanef, 1e9), axis=1, keepdims=True)
        picks = jnp.where(lane == it, first, picks)
        score = jnp.where(lanef == first, NEG, score)
    idx_ref[0] = picks.astype(jnp.int32)


def _decode_cmp(kc_raw, q, gk, seg1, pair, past_len):
    nseq, nbc, _ = kc_raw.shape
    smap = lambda b: (b, 0, 0)
    const = lambda a: pl.BlockSpec(a.shape, lambda b: (0, 0))
    return pl.pallas_call(
        functools.partial(_decode_cmp_kernel, past_len=past_len),
        grid=(nseq,),
        in_specs=[pl.BlockSpec((1, nbc, 256), smap), pl.BlockSpec((1, 1, 512), smap),
                  const(gk), const(seg1), const(pair)],
        out_specs=[pl.BlockSpec((1, 8, LANES), smap), pl.BlockSpec((1, 8, LANES), smap)],
        out_shape=[jax.ShapeDtypeStruct((nseq, 8, LANES), F32), jax.ShapeDtypeStruct((nseq, 8, LANES), jnp.int32)],
        compiler_params=_cparams(("parallel",)),
        name="decode_cmp",
    )(kc_raw, q, gk, seg1, pair)


N_PICK = SEL_TOPK - 1


def _decode_nsa_kernel(pt_ref, idx_ref, *refs, past_len):
    del pt_ref
    nblk = KVH_NSA * N_PICK
    blocks = refs[:nblk]
    q_ref, misc_ref, oc_ref, selnew_ref, win_ref, winnew_ref, o_ref, nwin_ref, kall_ref, vall_ref = refs[nblk:]
    b = pl.program_id(0)
    q8 = _nsa_q8(q_ref[0])
    q8b = q8.astype(BF16)
    slope = _slope_col()
    row8 = lax.broadcasted_iota(jnp.int32, (8, 1), 0)
    page = blocks[0].shape[3]
    pshift = page.bit_length() - 1
    lane = lax.broadcasted_iota(jnp.int32, (1, N_PICK * page), 1)
    slot = jnp.right_shift(lane, pshift)
    within = jnp.bitwise_and(lane, page - 1)
    new = selnew_ref[0]

    o_sel = jnp.zeros((8, LANES), F32)
    for g in range(KVH_NSA):
        blk_of_lane = jnp.zeros((1, N_PICK * page), jnp.int32)
        for j in range(N_PICK):
            pg = blocks[g * N_PICK + j][0, 0]
            kall_ref[:, j * page:(j + 1) * page] = pg[:LANES].astype(BF16)
            vall_ref[:, j * page:(j + 1) * page] = pg[LANES:].astype(BF16)
            blk_of_lane = jnp.where(slot == j, idx_ref[(b * 8 + g) * LANES + j], blk_of_lane)
        pos = jnp.left_shift(jnp.right_shift(blk_of_lane * SEL_BLOCK, pshift), pshift) + within
        ok = (jnp.right_shift(pos, 6) == blk_of_lane) & (pos <= past_len)
        s = jnp.where(ok, _dot(q8b, kall_ref[...]) - slope * (past_len - pos).astype(F32), NEG)
        s_new = jnp.sum(q8 * new[:, :LANES], axis=1, keepdims=True)
        m = jnp.maximum(jnp.max(s, axis=1, keepdims=True), s_new)
        p = jnp.where(ok, jnp.exp(s - m), 0.0)
        p_new = jnp.exp(s_new - m)
        og = (_dot_nt(p.astype(BF16), vall_ref[...]) + p_new * new[:, LANES:]) / (
            jnp.sum(p, axis=1, keepdims=True) + p_new)
        o_sel = jnp.where(jnp.right_shift(row8, 2) == g, og, o_sel)

    kw = win_ref[0, 0]
    nbuf = kw.shape[0]
    wlane = lax.broadcasted_iota(jnp.int32, (1, nbuf), 1)
    dw = nbuf - wlane
    okw = (dw <= WINDOW) & (past_len - dw >= 0)
    wnew = winnew_ref[0]
    s = jnp.where(okw, _dot_nt(q8b, kw[:, :LANES].astype(BF16)) - slope * dw.astype(F32), NEG)
    s_new = jnp.sum(q8 * wnew[:, :LANES], axis=1, keepdims=True)
    m = jnp.maximum(jnp.max(s, axis=1, keepdims=True), s_new)
    p = jnp.where(okw, jnp.exp(s - m), 0.0)
    p_new = jnp.exp(s_new - m)
    o_win = (_dot(p.astype(BF16), kw[:, LANES:].astype(BF16)) + p_new * wnew[:, LANES:]) / (
        jnp.sum(p, axis=1, keepdims=True) + p_new)
    rows = lax.broadcasted_iota(jnp.int32, (nbuf, 1), 0)
    nwin_ref[0] = jnp.where(rows == nbuf - 1, jnp.broadcast_to(wnew, kw.shape), pltpu.roll(kw, nbuf - 1, 0))

    mrow = jnp.broadcast_to(misc_ref[0], (8, LANES))
    glane = lax.broadcasted_iota(jnp.int32, (8, LANES), 1)
    gate = [jnp.sum(jnp.where(glane == 3 * row8 + br, mrow, 0.0), axis=1, keepdims=True) for br in range(3)]
    out = gate[0] * oc_ref[0] + gate[1] * o_sel + gate[2] * o_win
    low = _lane_lt64((1, LANES))
    for r in range(NSA_REP):
        o_ref[0, :, r * LANES:(r + 1) * LANES] = jnp.where(low, out[r:r + 1], out[NSA_REP + r:NSA_REP + r + 1])


def _decode_nsa(pt_flat, idx_flat, layer, sel_pool, q, misc, oc, selnew, win_buf, winnew, npages, past_len):
    nseq = q.shape[0]
    nbuf = win_buf.shape[2]
    page = sel_pool.shape[3]
    assert page & (page - 1) == 0 and page % SEL_BLOCK == 0
    per_page = page // SEL_BLOCK
    smap = lambda b, pt, idx: (b, 0, 0)

    def pick(g, j):
        def index_map(b, pt, idx):
            bb = jnp.minimum(b, nseq - 1)
            blk = jnp.clip(idx[(bb * 8 + g) * LANES + j], 0, npages * per_page - 1)
            return (layer, pt[bb * npages + blk // per_page], 0, 0)
        return pl.BlockSpec((1, 1, 256, page), index_map)

    grid_spec = pltpu.PrefetchScalarGridSpec(
        num_scalar_prefetch=2,
        grid=(nseq,),
        in_specs=[pick(g, j) for g in range(KVH_NSA) for j in range(N_PICK)]
        + [pl.BlockSpec((1, 1, 512), smap), pl.BlockSpec((1, 1, LANES), smap), pl.BlockSpec((1, 8, LANES), smap),
           pl.BlockSpec((1, 1, 256), smap), pl.BlockSpec((1, 1, nbuf, 256), lambda b, pt, idx: (layer, b, 0, 0)),
           pl.BlockSpec((1, 1, 256), smap)],
        out_specs=[pl.BlockSpec((1, 1, 512), smap), pl.BlockSpec((1, nbuf, 256), smap)],
        scratch_shapes=[pltpu.VMEM((LANES, N_PICK * page), BF16), pltpu.VMEM((LANES, N_PICK * page), BF16)],
    )
    return pl.pallas_call(
        functools.partial(_decode_nsa_kernel, past_len=past_len),
        grid_spec=grid_spec,
        out_shape=[jax.ShapeDtypeStruct((nseq, 1, 512), F32), jax.ShapeDtypeStruct((nseq, nbuf, 256), F32)],
        compiler_params=_cparams(("parallel",)),
        name="decode_nsa",
    )(pt_flat, idx_flat, *([sel_pool] * (KVH_NSA * N_PICK)), q, misc, oc, selnew, win_buf, winnew)


NSA_HEAD_ORDER = (0, 4, 1, 5, 2, 6, 3, 7)


def _layer_params(l, g_mix_norm, w_in, b_nsa_gate, b_forget, g_qk, w_cmp, g_head_out, w_out, g_mem_norm,
                  w_mem_q, w_mem_kv, g_mem_qk, w_mem_o, g_mlp_norm, w_up, w_down):
    order = jnp.asarray(NSA_HEAD_ORDER)
    wi = w_in[l]
    wq = wi[:, :512].reshape(D_MODEL, H_NSA, HEAD_DIM)[:, order].reshape(D_MODEL, 512)
    pad = jnp.zeros((D_MODEL, LANES - GATE_COLS - H_FOX), F32)
    w_in_p = jnp.concatenate([wq, wi[:, 512:1280], wi[:, 1304:2840], wi[:, 1280:1304], wi[:, 2840:2844], pad],
                             axis=1).astype(BF16)
    bias = jnp.concatenate([b_nsa_gate[l], b_forget[l], jnp.zeros((LANES - GATE_COLS - H_FOX,), F32)])[None]
    gqk = g_qk[l]
    tile = lambda v, n: jnp.tile(v, n)[None]
    head_order = jnp.concatenate([order, jnp.arange(H_NSA, H_NSA + H_SB + H_FOX)])
    return dict(
        g_mix=g_mix_norm[l][None], w_in=w_in_p, bias=bias,
        gq=tile(gqk[0], 4), gsel=tile(gqk[2], 2), gwin=tile(gqk[3], 2), gfq=tile(gqk[4], 4), gfk=tile(gqk[5], 4),
        w_cmp=jnp.repeat(jnp.transpose(w_cmp[l], (1, 0, 2)).reshape(CMP_BLOCK, 4), HEAD_DIM, axis=1),
        gk_cmp=tile(gqk[1], 2),
        g_head=g_head_out[l].reshape(-1, HEAD_DIM)[head_order].reshape(1, -1),
        w_out=w_out[l].reshape(-1, HEAD_DIM, D_MODEL)[head_order].reshape(-1, D_MODEL).astype(BF16),
        g_mem=g_mem_norm[l][None], w_mem_q=w_mem_q[l].astype(BF16), w_mem_kv=w_mem_kv[l].astype(BF16),
        gq_mem=tile(g_mem_qk[l, 0], 4), gk_mem=tile(g_mem_qk[l, 1], 4), w_mem_o=w_mem_o[l].astype(BF16),
        g_mlp=g_mlp_norm[l][None], w_up=w_up[l].astype(BF16), w_down=w_down[l].astype(BF16),
    )


def _prompt_consts(seq):
    nbc = seq // CMP_BLOCK
    n = jnp.arange(nbc)
    pair = (jnp.arange(LANES)[:, None] == n[None, :] // 2).astype(BF16)
    kt = jnp.arange(seq // TK_SEL)[:, None, None]
    j = jnp.arange(LANES)[None, :, None]
    c = jnp.arange(TK_SEL)[None, None, :]
    expand = (j == kt * (TK_SEL // SEL_BLOCK) + c // SEL_BLOCK).astype(BF16)
    return dict(seg=_seg_matrix(256), pair=pair, expand=expand, upper=_strict_upper(TQ_ATT),
                tri=_lower_incl(CUMSUM_CHUNK))


def _prompt_layer(x, mem_rows, p, c, batch, seq):
    seg = c["seg"]
    (q, cmp, sel, selb, win, winb, misc, qsb, sb, sbb, qfox, fox, foxb) = _in_proj(
        x, p["g_mix"], p["w_in"], p["bias"], p["gq"], p["gsel"], p["gwin"], p["gfq"], p["gfk"], seg, 512, seq)
    kc, vc = _compress(cmp, p["w_cmp"], p["gk_cmp"], seg[:LANES, :LANES], 1024)
    nbc = seq // CMP_BLOCK
    o_nsa = _nsa_prompt(q, misc, kc.reshape(batch, nbc, LANES), vc.reshape(batch, nbc, LANES), selb, winb,
                        c["pair"], c["expand"], batch, seq)
    o_sb = _sb_prompt(qsb, sbb, c["upper"], batch, seq)
    c_rows = _cumsum_rows(misc, c["tri"], batch, seq)
    nq = seq // TQ_ATT
    c_lanes = jnp.transpose(c_rows[:, GATE_COLS:GATE_COLS + 8].reshape(batch, nq, TQ_ATT, 8), (0, 1, 3, 2))
    o_fox = _fox_prompt(qfox, foxb, c_rows, c_lanes, batch, seq)
    x = _out_proj(o_nsa, o_sb, o_fox, p["g_head"], p["w_out"], x, seg, 512)
    kvm, kvmb = _mem_kv(mem_rows, p["w_mem_kv"], p["gk_mem"], seg, 256)
    qm = _mem_q(x, p["g_mem"], p["w_mem_q"], p["gq_mem"], seg, 512, BF16)
    x = _mem_attn(qm.reshape(batch, seq, 256), kvmb.reshape(1, batch, N_MEM, 512), 0, p["w_mem_o"],
                  x.reshape(batch, seq, D_MODEL), 512).reshape(batch * seq, D_MODEL)
    x = _mlp(x, p["g_mlp"], p["w_up"], p["w_down"], 1024)
    kv5t = lambda a, h: jnp.moveaxis(a.reshape(batch, 2, h, HEAD_DIM, -1), -1, 1)
    w_keep = min(WINDOW, seq)
    state = (kv5t(cmp, KVH_NSA), kv5t(sel, KVH_NSA), kv5t(win[:, :, seq - w_keep:], KVH_NSA), kv5t(sb, H_SB),
             kv5t(fox, H_FOX), misc[:, GATE_COLS:GATE_COLS + H_FOX].reshape(batch, seq, H_FOX),
             kvm.reshape(batch, -1, 2, H_MEM, HEAD_DIM))
    return x, state


def _sample_layer(x, pools, layer, p, c, pt_flat, npages, past_len):
    cmp_pool, sel_pool, win_buf, sb_pool, fox_pool, lf_pool, mem_kv = pools
    nseq = x.shape[0]
    seg = c["seg"]
    seg1 = seg[:LANES, :LANES]
    (q, cmp, sel, _, win, _, misc, qsb, sb, _, qfox, fox, _) = _in_proj(
        x, p["g_mix"], p["w_in"], p["bias"], p["gq"], p["gsel"], p["gwin"], p["gfq"], p["gfk"], seg, nseq)
    r3 = lambda a: a.astype(F32).reshape(nseq, 1, -1)
    logf = misc[:, GATE_COLS:GATE_COLS + H_FOX]
    lfnew = jnp.broadcast_to(jnp.pad(logf, ((0, 0), (0, 8 - H_FOX)))[:, :, None], (nseq, 8, LANES))
    page = cmp_pool.shape[3]
    w_cmp_t = jnp.tile(p["w_cmp"].T, (1, page // CMP_BLOCK))
    kc4, o_sb, o_fox = _decode_pages(pt_flat, layer, cmp_pool, sb_pool, fox_pool, lf_pool, r3(qsb), r3(qfox),
                                     r3(fox), lfnew, w_cmp_t, c["upper_page"], nseq, npages)
    kc_raw = kc4.reshape(nseq, -1, 256)
    oc, idx = _decode_cmp(kc_raw, r3(q), p["gk_cmp"], seg1, c["pair_dec"], past_len)
    o_nsa, new_win = _decode_nsa(pt_flat, idx.reshape(-1), layer, sel_pool, r3(q), r3(misc), oc, r3(sel), win_buf,
                                 r3(win), npages, past_len)
    x = _out_proj(o_nsa.reshape(nseq, 512), o_sb.reshape(nseq, 256), o_fox.reshape(nseq, 256),
                  p["g_head"], p["w_out"], x, seg, nseq)
    qm = _mem_q(x, p["g_mem"], p["w_mem_q"], p["gq_mem"], seg, nseq, F32)
    rep = lambda a: jnp.broadcast_to(a[:, None, :], (nseq, 8, a.shape[-1]))
    x = _mem_attn(rep(qm), mem_kv, layer, p["w_mem_o"], rep(x), 8)[:, 0]
    x = _mlp(x, p["g_mlp"], p["w_up"], p["w_down"], nseq)
    kv5 = lambda a, h: a.reshape(nseq, -1, 2, h, HEAD_DIM)
    state = (kv5(cmp, KVH_NSA), kv5(sel, KVH_NSA), kv5(new_win, KVH_NSA), kv5(sb, H_SB), kv5(fox, H_FOX),
             logf.reshape(nseq, 1, H_FOX))
    return x, state


def _decode_consts(npages, page):
    nbc = npages * page // CMP_BLOCK
    pair = (jnp.arange(nbc)[:, None] // 2 == jnp.arange(LANES)[None, :]).astype(BF16)
    return dict(seg=_seg_matrix(256), upper_page=_strict_upper(page), pair_dec=pair)


def kernel(x_prompt, x_sample, mem_prompt, cache_nsa_cmp, cache_nsa_sel, cache_nsa_win, cache_sb, cache_fox,
           cache_fox_logf, cache_mem, page_table, g_mix_norm, w_in, b_nsa_gate, b_forget, g_qk, w_cmp,
           g_head_out, w_out, g_mem_norm, w_mem_q, w_mem_kv, g_mem_qk, w_mem_o, g_mlp_norm, w_up, w_down):
    batch, seq, _ = x_prompt.shape
    nseq, dec_seq, _ = x_sample.shape
    assert dec_seq == 1, "the decode kernels take one new token per sequence"
    depth, n_pool, page = cache_sb.shape[:3]
    npages = page_table.shape[1]
    past_len = npages * page
    assert past_len // SEL_BLOCK <= LANES and cache_nsa_win.shape[2] <= past_len and npages % PAGES_PER_STEP == 0
    pconsts = _prompt_consts(seq)
    dconsts = _decode_consts(npages, page)
    pt_flat = page_table.reshape(-1).astype(jnp.int32)
    paged = lambda a, width: jnp.swapaxes(a.reshape(depth, n_pool, page, width), 2, 3)
    cmp_pool = paged(cache_nsa_cmp, 256)
    sel_pool = paged(cache_nsa_sel, 256)
    win_buf = cache_nsa_win.reshape(depth, nseq, -1, 256)
    sb_pool = paged(cache_sb, 512)
    fox_pool = paged(cache_fox, 512)
    lf_pool = jnp.pad(jnp.transpose(cache_fox_logf, (0, 1, 3, 2)), ((0, 0), (0, 0), (0, 8 - H_FOX), (0, 0)))
    mem_kv = cache_mem.reshape(depth, nseq, N_MEM, 512)

    xp = x_prompt.reshape(batch * seq, D_MODEL)
    xs = x_sample.reshape(nseq, D_MODEL)
    mem_rows = mem_prompt.reshape(batch * N_MEM, D_MODEL)
    pools = (cmp_pool, sel_pool, win_buf, sb_pool, fox_pool, lf_pool, mem_kv)
    pstates, sstates = [], []
    for l in range(depth):
        p = _layer_params(l, g_mix_norm, w_in, b_nsa_gate, b_forget, g_qk, w_cmp, g_head_out, w_out, g_mem_norm,
                          w_mem_q, w_mem_kv, g_mem_qk, w_mem_o, g_mlp_norm, w_up, w_down)
        xp, st = _prompt_layer(xp, mem_rows, p, pconsts, batch, seq)
        pstates.append(st)
        xs, st = _sample_layer(xs, pools, l, p, dconsts, pt_flat, npages, past_len)
        sstates.append(st)
    stack = lambda states: tuple(jnp.stack([s[k] for s in states], axis=0) for k in range(len(states[0])))
    return (xp.reshape(batch, seq, D_MODEL), xs.reshape(nseq, 1, D_MODEL)) + stack(pstates) + stack(sstates)
```
